```python
import math
import jax
import jax.numpy as jnp
from jax import lax
import numpy as np

D_MODEL = 1024
BATCH = 32
SEQ = 2048
DEPTH = 4

CHUNK = 64
EPS = 1e-6
GROUP_WIDTH = 256
N_MIXERS = 4
MIX_WIDTH = GROUP_WIDTH * N_MIXERS
SSD_HEADS = 4
SSD_HEAD_DIM = 64
SSD_INNER = SSD_HEADS * SSD_HEAD_DIM
SSD_GROUPS = 2
SSD_STATE = 64
SSD_CONV = 4
SSD_CONV_DIM = SSD_INNER + 2 * SSD_GROUPS * SSD_STATE
GDN_HEADS = 4
GDN_HEAD_DIM = 64
GDN_INNER = GDN_HEADS * GDN_HEAD_DIM
GDN_CONV = 4
GLA_HEADS = 4
GLA_KEY_DIM = 32
GLA_VAL_DIM = 64
GLA_KEY_WIDTH = GLA_HEADS * GLA_KEY_DIM
GLA_VAL_WIDTH = GLA_HEADS * GLA_VAL_DIM
GLA_GATE_RANK = 16
GLA_GATE_NORMALIZER = 16.0
DSA_HEADS = 4
DSA_HEAD_DIM = 64
IDX_HEADS = 8
IDX_DIM = 32
DSA_TOPK_MAX = 256
DSA_QBLOCK = 128
MEM_TOKENS = 256
MEM_HEADS = 4
MEM_HEAD_DIM = 64
MEM_WIDTH = MEM_HEADS * MEM_HEAD_DIM
PEER_HEADS = 8
PEER_NKEYS = 128
PEER_EXPERTS = PEER_NKEYS * PEER_NKEYS
PEER_QDIM = 128
PEER_TOPK = 16
PEER_BLOCK = 32

IN_SIZES = (
    SSD_INNER, SSD_CONV_DIM, SSD_HEADS,
    3 * GDN_INNER, GDN_HEADS, GDN_HEADS, GDN_INNER,
    GLA_KEY_WIDTH, GLA_KEY_WIDTH, GLA_VAL_WIDTH, GLA_GATE_RANK, GLA_VAL_WIDTH,
    DSA_HEADS * DSA_HEAD_DIM, DSA_HEAD_DIM, DSA_HEAD_DIM,
    IDX_HEADS * IDX_DIM, IDX_DIM, IDX_HEADS,
)
IN_WIDTH = sum(IN_SIZES)

kernel_name = "hybrid_chunk_causal_trunk"


def rms_norm(x, g):
    xf = x.astype(jnp.float32)
    y = xf * lax.rsqrt(jnp.mean(xf * xf, axis=-1, keepdims=True) + EPS)
    return (y * g.astype(jnp.float32)).astype(x.dtype)


def l2_norm(x):
    xf = x.astype(jnp.float32)
    return xf * lax.rsqrt(jnp.sum(xf * xf, axis=-1, keepdims=True) + EPS)


def causal_dwconv(x, w):
    k = w.shape[0]
    return lax.conv_general_dilated(
        x, w[:, None, :].astype(x.dtype), window_strides=(1,), padding=[(k - 1, 0)],
        dimension_numbers=("NWC", "WIO", "NWC"), feature_group_count=x.shape[-1])


def to_chunks(t):
    bsz, seq, nh = t.shape[:3]
    t = t.reshape((bsz, seq // CHUNK, CHUNK, nh) + t.shape[3:])
    return jnp.moveaxis(jnp.moveaxis(t, 3, 2), 1, 0)


def from_chunks(t):
    nc, bsz, nh, q = t.shape[:4]
    t = jnp.moveaxis(jnp.moveaxis(t, 0, 1), 2, 3)
    return t.reshape((bsz, nc * q, nh) + t.shape[4:])


def ssd_scan(xs, dt, a, bm, cm):
    bsz, seq, nh, hp = xs.shape
    nc = seq // CHUNK
    xc = xs.reshape(bsz, nc, CHUNK, nh, hp)
    bc = bm.reshape(bsz, nc, CHUNK, nh, -1)
    cc = cm.reshape(bsz, nc, CHUNK, nh, -1)
    dtc = jnp.moveaxis(dt.reshape(bsz, nc, CHUNK, nh), 2, 3)
    acum = jnp.cumsum(dtc * a[:, None], axis=-1)
    causal = jnp.tril(jnp.ones((CHUNK, CHUNK), bool))
    seg = jnp.exp(jnp.where(causal, acum[..., :, None] - acum[..., None, :], -jnp.inf))
    cb = jnp.einsum("bcthn,bcshn->bchts", cc, bc)
    y_diag = jnp.einsum("bchts,bcshp->bcthp", cb * seg * dtc[..., None, :], xc)
    decay_end = jnp.exp(acum[..., -1:] - acum) * dtc
    states = jnp.einsum("bcshn,bchs,bcshp->bchpn", bc, decay_end, xc)
    chunk_decay = jnp.exp(acum[..., -1])

    def step(h, inp):
        st, dec = inp
        return h * dec[..., None, None] + st, h

    h0 = jnp.zeros((bsz, nh, hp, bc.shape[-1]), states.dtype)
    _, h_in = lax.scan(step, h0, (jnp.moveaxis(states, 1, 0), jnp.moveaxis(chunk_decay, 1, 0)))
    h_in = jnp.moveaxis(h_in, 0, 1)
    y_off = jnp.einsum("bcthn,bchpn,bcht->bcthp", cc, h_in, jnp.exp(acum))
    return (y_diag + y_off).reshape(bsz, seq, nh, hp)


def ssd_mixer(z, xbc, dt, conv_w, conv_b, dt_bias, a_log, d_skip, norm_g):
    bsz, seq, _ = z.shape
    f32 = jnp.float32
    xbc = jax.nn.silu(causal_dwconv(xbc, conv_w) + conv_b).astype(f32)
    gs = SSD_GROUPS * SSD_STATE
    rep = SSD_HEADS // SSD_GROUPS
    xs = xbc[..., :SSD_INNER].reshape(bsz, seq, SSD_HEADS, SSD_HEAD_DIM)
    bm = jnp.repeat(xbc[..., SSD_INNER:SSD_INNER + gs].reshape(bsz, seq, SSD_GROUPS, SSD_STATE), rep, axis=2)
    cm = jnp.repeat(xbc[..., SSD_INNER + gs:].reshape(bsz, seq, SSD_GROUPS, SSD_STATE), rep, axis=2)
    dt = jax.nn.softplus(dt.astype(f32) + dt_bias.astype(f32))
    a = -jnp.exp(a_log.astype(f32))
    y = ssd_scan(xs, dt, a, bm, cm) + xs * d_skip.astype(f32)[:, None]
    y = y.reshape(bsz, seq, SSD_INNER) * jax.nn.silu(z.astype(f32))
    return rms_norm(y, norm_g).astype(z.dtype)


def gated_delta_scan(q, k, v, beta, g):
    bsz, seq, nh, dk = q.shape
    dv = v.shape[-1]
    qc, kc, vc = to_chunks(q), to_chunks(k), to_chunks(v)
    bc, gcum = to_chunks(beta), jnp.cumsum(to_chunks(g), axis=-1)
    incl = jnp.tril(jnp.ones((CHUNK, CHUNK), bool))
    strict = jnp.tril(jnp.ones((CHUNK, CHUNK), bool), k=-1)
    decay = jnp.exp(jnp.where(incl, gcum[..., :, None] - gcum[..., None, :], -jnp.inf))
    kb = kc * bc[..., None]
    a_mat = jnp.where(strict, jnp.einsum("nbhid,nbhjd->nbhij", kb, kc) * decay, 0.0)
    eye = jnp.eye(CHUNK, dtype=jnp.float32)
    t_mat = lax.linalg.triangular_solve(eye + a_mat, jnp.broadcast_to(eye, a_mat.shape),
                                        left_side=True, lower=True)
    u = jnp.einsum("nbhij,nbhjd->nbhid", t_mat, vc * bc[..., None])
    w = jnp.einsum("nbhij,nbhjd->nbhid", t_mat, kb * jnp.exp(gcum)[..., None])
    qk = jnp.where(incl, jnp.einsum("nbhid,nbhjd->nbhij", qc, kc) * decay, 0.0)
    q_dec = qc * jnp.exp(gcum)[..., None]
    k_dec = kc * jnp.exp(gcum[..., -1:] - gcum)[..., None]
    last = jnp.exp(gcum[..., -1])

    def step(s, inp):
        u_c, w_c, qk_c, qd_c, kd_c, l_c = inp
        v_new = u_c - jnp.einsum("bhid,bhde->bhie", w_c, s)
        o = jnp.einsum("bhid,bhde->bhie", qd_c, s) + jnp.einsum("bhij,bhje->bhie", qk_c, v_new)
        s = s * l_c[..., None, None] + jnp.einsum("bhjd,bhje->bhde", kd_c, v_new)
        return s, o

    s0 = jnp.zeros((bsz, nh, dk, dv), jnp.float32)
    _, o = lax.scan(step, s0, (u, w, qk, q_dec, k_dec, last))
    return from_chunks(o)


def gdn_mixer(qkv, a, b, gate, conv_w, a_log, dt_bias, norm_g):
    bsz, seq, _ = qkv.shape
    f32 = jnp.float32
    out_dtype = qkv.dtype
    shp = (bsz, seq, GDN_HEADS, GDN_HEAD_DIM)
    qkv = jax.nn.silu(causal_dwconv(qkv, conv_w)).astype(f32)
    q = l2_norm(qkv[..., :GDN_INNER].reshape(shp)) * GDN_HEAD_DIM ** -0.5
    k = l2_norm(qkv[..., GDN_INNER:2 * GDN_INNER].reshape(shp))
    v = qkv[..., 2 * GDN_INNER:].reshape(shp)
    beta = jax.nn.sigmoid(b.astype(f32))
    g = -jnp.exp(a_log.astype(f32)) * jax.nn.softplus(a.astype(f32) + dt_bias.astype(f32))
    o = gated_delta_scan(q, k, v, beta, g)
    o = rms_norm(o, norm_g) * jax.nn.silu(gate.astype(f32).reshape(shp))
    return o.reshape(bsz, seq, GDN_INNER).astype(out_dtype)


def gla_scan(q, k, v, gk):
    bsz, seq, nh, dk = q.shape
    dv = v.shape[-1]
    gcum = jnp.cumsum(to_chunks(gk), axis=-2)
    incl = jnp.tril(jnp.ones((CHUNK, CHUNK), bool))[:, :, None]

    def step(s, inp):
        qc, kc, vc, gc = inp
        diff = gc[:, :, :, None, :] - gc[:, :, None, :, :]
        dec = jnp.exp(jnp.where(incl, diff, -jnp.inf))
        att = jnp.einsum("bhtd,bhsd,bhtsd->bhts", qc, kc, dec)
        o = jnp.einsum("bhts,bhse->bhte", att, vc) + jnp.einsum("bhtd,bhde->bhte", qc * jnp.exp(gc), s)
        s = s * jnp.exp(gc[:, :, -1, :])[..., None] + jnp.einsum(
            "bhsd,bhse->bhde", kc * jnp.exp(gc[:, :, -1:, :] - gc), vc)
        return s, o

    s0 = jnp.zeros((bsz, nh, dk, dv), jnp.float32)
    _, o = lax.scan(step, s0, (to_chunks(q), to_chunks(k), to_chunks(v), gcum))
    return from_chunks(o)


def gla_mixer(q, k, v, glr, r, w_gate2, b_gate, norm_g):
    bsz, seq, _ = q.shape
    f32 = jnp.float32
    kshape = (bsz, seq, GLA_HEADS, GLA_KEY_DIM)
    vshape = (bsz, seq, GLA_HEADS, GLA_VAL_DIM)
    gk = jax.nn.log_sigmoid((glr @ w_gate2 + b_gate).astype(f32)) / GLA_GATE_NORMALIZER
    o = gla_scan(q.astype(f32).reshape(kshape) * GLA_KEY_DIM ** -0.5, k.astype(f32).reshape(kshape),
                 v.astype(f32).reshape(vshape), gk.reshape(kshape))
    o = rms_norm(o, norm_g) * jax.nn.silu(r.astype(f32).reshape(vshape))
    return o.reshape(bsz, seq, GLA_VAL_WIDTH).astype(q.dtype)


def dsa_mixer(q, k, v, iq, ik, iw, q_norm, k_norm, ik_norm):
    bsz, seq, _ = q.shape
    q = rms_norm(q.reshape(bsz, seq, DSA_HEADS, DSA_HEAD_DIM), q_norm)
    k = rms_norm(k, k_norm)
    iq = iq.reshape(bsz, seq, IDX_HEADS, IDX_DIM)
    ik = rms_norm(ik, ik_norm)
    iw = iw * (IDX_HEADS ** -0.5 * IDX_DIM ** -0.5)
    topk = min(DSA_TOPK_MAX, seq // 4)
    key_chunk = jnp.arange(seq) // CHUNK
    gather = jax.vmap(lambda t, i: t[i])

    def block(i):
        start = i * DSA_QBLOCK
        qb = lax.dynamic_slice_in_dim(q, start, DSA_QBLOCK, axis=1)
        iqb = lax.dynamic_slice_in_dim(iq, start, DSA_QBLOCK, axis=1)
        iwb = lax.dynamic_slice_in_dim(iw, start, DSA_QBLOCK, axis=1)
        q_chunk = (start + jnp.arange(DSA_QBLOCK)) // CHUNK
        score = jnp.einsum("bths,bth->bts", jax.nn.relu(jnp.einsum("bthd,bsd->bths", iqb, ik)),
                           iwb).astype(jnp.float32)
        score = jnp.where(key_chunk[None, None, :] <= q_chunk[None, :, None], score, -jnp.inf)
        _, idx = lax.top_k(score, topk)
        valid = key_chunk[idx] <= q_chunk[None, :, None]
        kg = gather(k, idx)
        vg = gather(v, idx)
        s = jnp.einsum("bthd,btkd->bhtk", qb, kg).astype(jnp.float32) * DSA_HEAD_DIM ** -0.5
        s = jnp.where(valid[:, None], s, -jnp.inf)
        p = jax.nn.softmax(s, axis=-1).astype(v.dtype)
        return jnp.einsum("bhtk,btkd->bthd", p, vg)

    out = lax.map(block, jnp.arange(seq // DSA_QBLOCK))
    return jnp.moveaxis(out, 0, 1).reshape(bsz, seq, DSA_HEADS * DSA_HEAD_DIM)


def memory_xattn(h, m, wq, wk, wv, wo, q_norm, k_norm):
    bsz, seq, _ = h.shape
    nm = m.shape[1]
    q = rms_norm((h @ wq).reshape(bsz, seq, MEM_HEADS, MEM_HEAD_DIM), q_norm)
    k = rms_norm((m @ wk).reshape(bsz, nm, MEM_HEADS, MEM_HEAD_DIM), k_norm)
    v = (m @ wv).reshape(bsz, nm, MEM_HEADS, MEM_HEAD_DIM)
    s = jnp.einsum("bthd,bshd->bhts", q, k).astype(jnp.float32) * MEM_HEAD_DIM ** -0.5
    p = jax.nn.softmax(s, axis=-1).astype(v.dtype)
    o = jnp.einsum("bhts,bshd->bthd", p, v).reshape(bsz, seq, MEM_WIDTH)
    return o @ wo


def peer_ffn(h, w_query, sub_keys, u_tab, v_tab):
    bsz, seq, dm = h.shape
    half = PEER_QDIM // 2
    ncand = PEER_TOPK * PEER_TOPK

    def block(i):
        hb = lax.dynamic_slice_in_dim(h, i * PEER_BLOCK, PEER_BLOCK, axis=1)
        q = (hb @ w_query).reshape(bsz, PEER_BLOCK, PEER_HEADS, 2, half)
        s = jnp.einsum("bthpd,hpnd->bthpn", q, sub_keys).astype(jnp.float32)
        s1, i1 = lax.top_k(s[..., 0, :], PEER_TOPK)
        s2, i2 = lax.top_k(s[..., 1, :], PEER_TOPK)
        cand = (s1[..., :, None] + s2[..., None, :]).reshape(bsz, PEER_BLOCK, PEER_HEADS, ncand)
        cidx = (i1[..., :, None] * PEER_NKEYS + i2[..., None, :]).reshape(bsz, PEER_BLOCK, PEER_HEADS, ncand)
        top, pos = lax.top_k(cand, PEER_TOPK)
        eidx = jnp.take_along_axis(cidx, pos, axis=-1)
        gate = jax.nn.softmax(top, axis=-1)
        act = jax.nn.gelu(jnp.einsum("bthkd,btd->bthk", u_tab[eidx], hb).astype(jnp.float32))
        coef = (gate * act).astype(hb.dtype)
        return jnp.einsum("bthk,bthkd->btd", coef, v_tab[eidx])

    out = lax.map(block, jnp.arange(seq // PEER_BLOCK))
    return jnp.moveaxis(out, 0, 1).reshape(bsz, seq, dm)


def _split_points():
    return np.cumsum(np.array(IN_SIZES))[:-1].tolist()


def setup_inputs(seed: int = 0) -> dict:
    key = jax.random.key(seed)
    keys = iter(jax.random.split(key, 48))
    f32 = jnp.float32

    def nrm(shape, scale):
        return scale * jax.random.normal(next(keys), shape, f32)

    def gain(shape):
        return 1.0 + 0.02 * jax.random.normal(next(keys), shape, f32)

    def dt_bias_init(shape):
        dt = jnp.exp(jax.random.uniform(next(keys), shape, f32, math.log(1e-3), math.log(1e-1)))
        return dt + jnp.log(-jnp.expm1(-dt))

    def a_log_init(shape):
        return jnp.log(jax.random.uniform(next(keys), shape, f32, 1.0, 16.0))

    L = DEPTH
    return {
        "x": nrm((BATCH, SEQ, D_MODEL), 1.0),
        "mem": nrm((BATCH, MEM_TOKENS, D_MODEL), 1.0),
        "mix_norm": gain((L, D_MODEL)),
        "w_in": nrm((L, D_MODEL, IN_WIDTH), D_MODEL ** -0.5),
        "ssd_conv_w": nrm((L, SSD_CONV, SSD_CONV_DIM), SSD_CONV ** -0.5),
        "ssd_conv_b": nrm((L, SSD_CONV_DIM), 0.01),
        "ssd_dt_bias": dt_bias_init((L, SSD_HEADS)),
        "ssd_a_log": a_log_init((L, SSD_HEADS)),
        "ssd_d": gain((L, SSD_HEADS)),
        "ssd_norm": gain((L, SSD_INNER)),
        "gdn_conv_w": nrm((L, GDN_CONV, 3 * GDN_INNER), GDN_CONV ** -0.5),
        "gdn_a_log": a_log_init((L, GDN_HEADS)),
        "gdn_dt_bias": dt_bias_init((L, GDN_HEADS)),
        "gdn_norm": gain((L, GDN_HEAD_DIM)),
        "gla_w_gate": nrm((L, GLA_GATE_RANK, GLA_KEY_WIDTH), GLA_GATE_RANK ** -0.5),
        "gla_b_gate": nrm((L, GLA_KEY_WIDTH), 0.1),
        "gla_norm": gain((L, GLA_VAL_DIM)),
        "dsa_q_norm": gain((L, DSA_HEAD_DIM)),
        "dsa_k_norm": gain((L, DSA_HEAD_DIM)),
        "idx_k_norm": gain((L, IDX_DIM)),
        "w_out": nrm((L, MIX_WIDTH, D_MODEL), 0.5 * MIX_WIDTH ** -0.5),
        "xattn_norm": gain((L, D_MODEL)),
        "mem_norm": gain((L, D_MODEL)),
        "xattn_wq": nrm((L, D_MODEL, MEM_WIDTH), D_MODEL ** -0.5),
        "xattn_wk": nrm((L, D_MODEL, MEM_WIDTH), D_MODEL ** -0.5),
        "xattn_wv": nrm((L, D_MODEL, MEM_WIDTH), D_MODEL ** -0.5),
        "xattn_wo": nrm((L, MEM_WIDTH, D_MODEL), 0.5 * MEM_WIDTH ** -0.5),
        "xattn_q_norm": gain((L, MEM_HEAD_DIM)),
        "xattn_k_norm": gain((L, MEM_HEAD_DIM)),
        "ffn_norm": gain((L, D_MODEL)),
        "peer_w_query": nrm((L, D_MODEL, PEER_HEADS * PEER_QDIM), D_MODEL ** -0.5),
        "peer_sub_keys": nrm((L, PEER_HEADS, 2, PEER_NKEYS, PEER_QDIM // 2), (PEER_QDIM // 2) ** -0.5),
        "peer_u": nrm((L, PEER_EXPERTS, D_MODEL), D_MODEL ** -0.5),
        "peer_v": nrm((L, PEER_EXPERTS, D_MODEL), 0.5 * PEER_HEADS ** -0.5),
    }


def reference(x, mem, mix_norm, w_in, ssd_conv_w, ssd_conv_b, ssd_dt_bias, ssd_a_log, ssd_d, ssd_norm,
              gdn_conv_w, gdn_a_log, gdn_dt_bias, gdn_norm, gla_w_gate, gla_b_gate, gla_norm,
              dsa_q_norm, dsa_k_norm, idx_k_norm, w_out, xattn_norm, mem_norm, xattn_wq, xattn_wk,
              xattn_wv, xattn_wo, xattn_q_norm, xattn_k_norm, ffn_norm, peer_w_query, peer_sub_keys,
              peer_u, peer_v):
    splits = _split_points()
    for l in range(DEPTH):
        h = rms_norm(x, mix_norm[l])
        (ssd_z, ssd_xbc, ssd_dt, gdn_qkv, gdn_a, gdn_b, gdn_g, gla_q, gla_k, gla_v, gla_glr, gla_r,
         dsa_q, dsa_k, dsa_v, idx_q, idx_k, idx_w) = jnp.split(h @ w_in[l], splits, axis=-1)
        y_a = ssd_mixer(ssd_z, ssd_xbc, ssd_dt, ssd_conv_w[l], ssd_conv_b[l], ssd_dt_bias[l],
                        ssd_a_log[l], ssd_d[l], ssd_norm[l])
        y_b = gdn_mixer(gdn_qkv, gdn_a, gdn_b, gdn_g, gdn_conv_w[l], gdn_a_log[l], gdn_dt_bias[l], gdn_norm[l])
        y_c = gla_mixer(gla_q, gla_k, gla_v, gla_glr, gla_r, gla_w_gate[l], gla_b_gate[l], gla_norm[l])
        y_d = dsa_mixer(dsa_q, dsa_k, dsa_v, idx_q, idx_k, idx_w, dsa_q_norm[l], dsa_k_norm[l], idx_k_norm[l])
        x = x + jnp.concatenate([y_a, y_b, y_c, y_d], axis=-1).astype(x.dtype) @ w_out[l]
        x = x + memory_xattn(rms_norm(x, xattn_norm[l]), rms_norm(mem, mem_norm[l]), xattn_wq[l],
                             xattn_wk[l], xattn_wv[l], xattn_wo[l], xattn_q_norm[l], xattn_k_norm[l])
        x = x + peer_ffn(rms_norm(x, ffn_norm[l]), peer_w_query[l], peer_sub_keys[l], peer_u[l], peer_v[l])
    return x
```

```python
import functools
import math

import jax
import jax.numpy as jnp
import numpy as np
from jax import lax
from jax.experimental import pallas as pl
from jax.experimental.pallas import tpu as pltpu

CHUNK = 64
EPS = 1e-6
GROUP_WIDTH = 256
SSD_HEADS, SSD_HEAD_DIM, SSD_GROUPS, SSD_STATE, SSD_CONV = 4, 64, 2, 64, 4
SSD_INNER = SSD_HEADS * SSD_HEAD_DIM
SSD_CONV_DIM = SSD_INNER + 2 * SSD_GROUPS * SSD_STATE
GDN_HEADS, GDN_HEAD_DIM, GDN_CONV = 4, 64, 4
GDN_INNER = GDN_HEADS * GDN_HEAD_DIM
GLA_HEADS, GLA_KEY_DIM, GLA_VAL_DIM, GLA_GATE_RANK = 4, 32, 64, 16
GLA_KEY_WIDTH = GLA_HEADS * GLA_KEY_DIM
GLA_VAL_WIDTH = GLA_HEADS * GLA_VAL_DIM
GLA_GATE_NORMALIZER = 16.0
DSA_HEADS, DSA_HEAD_DIM, IDX_HEADS, IDX_DIM = 4, 64, 8, 32
DSA_TOPK_MAX, DSA_QBLOCK = 256, 128
MEM_HEADS, MEM_HEAD_DIM = 4, 64
MEM_WIDTH = MEM_HEADS * MEM_HEAD_DIM
PEER_HEADS, PEER_NKEYS, PEER_QDIM, PEER_TOPK, PEER_BLOCK = 8, 128, 128, 16, 32

IN_SIZES = (
    SSD_INNER, SSD_CONV_DIM, SSD_HEADS,
    3 * GDN_INNER, GDN_HEADS, GDN_HEADS, GDN_INNER,
    GLA_KEY_WIDTH, GLA_KEY_WIDTH, GLA_VAL_WIDTH, GLA_GATE_RANK, GLA_VAL_WIDTH,
    DSA_HEADS * DSA_HEAD_DIM, DSA_HEAD_DIM, DSA_HEAD_DIM,
    IDX_HEADS * IDX_DIM, IDX_DIM, IDX_HEADS,
)
IN_WIDTH = sum(IN_SIZES)

LANE = 128
VMEM_LIMIT = 48 * 1024 * 1024


def _round_up(n, m):
    return (n + m - 1) // m * m


def _mm_kernel(x_ref, g_ref, w_ref, *rest, normalize, has_res):
    if has_res:
        r_ref, o_ref, xs_ref = rest
    else:
        o_ref, xs_ref = rest

    @pl.when(pl.program_id(1) == 0)
    def _():
        xf = x_ref[...]
        if normalize:
            xf = xf * lax.rsqrt(jnp.mean(xf * xf, axis=-1, keepdims=True) + EPS) * g_ref[...]
        xs_ref[...] = xf.astype(jnp.bfloat16)

    acc = jnp.dot(xs_ref[...], w_ref[...], preferred_element_type=jnp.float32)
    if has_res:
        acc = acc + r_ref[...]
    o_ref[...] = acc


def _matmul(x, w, gain=None, residual=None, tm=512, tn=256):
    m, k = x.shape
    n = w.shape[1]
    n_pad = _round_up(n, tn)
    wb = w.astype(jnp.bfloat16)
    if n_pad != n:
        wb = jnp.pad(wb, ((0, 0), (0, n_pad - n)))
    normalize = gain is not None
    g = (gain if normalize else jnp.ones((k,), jnp.float32)).reshape(1, k).astype(jnp.float32)
    has_res = residual is not None
    assert m % tm == 0
    in_specs = [
        pl.BlockSpec((tm, k), lambda i, j: (i, 0)),
        pl.BlockSpec((1, k), lambda i, j: (0, 0)),
        pl.BlockSpec((k, tn), lambda i, j: (0, j)),
    ]
    args = [x, g, wb]
    if has_res:
        assert n_pad == n
        in_specs.append(pl.BlockSpec((tm, tn), lambda i, j: (i, j)))
        args.append(residual)
    out = pl.pallas_call(
        functools.partial(_mm_kernel, normalize=normalize, has_res=has_res),
        grid=(m // tm, n_pad // tn),
        in_specs=in_specs,
        out_specs=pl.BlockSpec((tm, tn), lambda i, j: (i, j)),
        out_shape=jax.ShapeDtypeStruct((m, n_pad), jnp.float32),
        scratch_shapes=[pltpu.VMEM((tm, k), jnp.bfloat16)],
        compiler_params=pltpu.CompilerParams(
            dimension_semantics=("arbitrary", "arbitrary"), vmem_limit_bytes=VMEM_LIMIT),
    )(*args)
    return out[:, :n] if n_pad != n else out


def rms_norm(x, g):
    xf = x.astype(jnp.float32)
    y = xf * lax.rsqrt(jnp.mean(xf * xf, axis=-1, keepdims=True) + EPS)
    return (y * g.astype(jnp.float32)).astype(x.dtype)


def l2_norm(x):
    xf = x.astype(jnp.float32)
    return xf * lax.rsqrt(jnp.sum(xf * xf, axis=-1, keepdims=True) + EPS)


def causal_dwconv(x, w):
    k = w.shape[0]
    return lax.conv_general_dilated(
        x, w[:, None, :].astype(x.dtype), window_strides=(1,), padding=[(k - 1, 0)],
        dimension_numbers=("NWC", "WIO", "NWC"), feature_group_count=x.shape[-1])


def to_chunks(t):
    bsz, seq, nh = t.shape[:3]
    t = t.reshape((bsz, seq // CHUNK, CHUNK, nh) + t.shape[3:])
    return jnp.moveaxis(jnp.moveaxis(t, 3, 2), 1, 0)


def from_chunks(t):
    nc, bsz, nh, q = t.shape[:4]
    t = jnp.moveaxis(jnp.moveaxis(t, 0, 1), 2, 3)
    return t.reshape((bsz, nc * q, nh) + t.shape[4:])


def ssd_scan(xs, dt, a, bm, cm):
    bsz, seq, nh, hp = xs.shape
    nc = seq // CHUNK
    xc = xs.reshape(bsz, nc, CHUNK, nh, hp)
    bc = bm.reshape(bsz, nc, CHUNK, nh, -1)
    cc = cm.reshape(bsz, nc, CHUNK, nh, -1)
    dtc = jnp.moveaxis(dt.reshape(bsz, nc, CHUNK, nh), 2, 3)
    acum = jnp.cumsum(dtc * a[:, None], axis=-1)
    causal = jnp.tril(jnp.ones((CHUNK, CHUNK), bool))
    seg = jnp.exp(jnp.where(causal, acum[..., :, None] - acum[..., None, :], -jnp.inf))
    cb = jnp.einsum("bcthn,bcshn->bchts", cc, bc)
    y_diag = jnp.einsum("bchts,bcshp->bcthp", cb * seg * dtc[..., None, :], xc)
    decay_end = jnp.exp(acum[..., -1:] - acum) * dtc
    states = jnp.einsum("bcshn,bchs,bcshp->bchpn", bc, decay_end, xc)
    chunk_decay = jnp.exp(acum[..., -1])

    def step(h, inp):
        st, dec = inp
        return h * dec[..., None, None] + st, h

    h0 = jnp.zeros((bsz, nh, hp, bc.shape[-1]), states.dtype)
    _, h_in = lax.scan(step, h0, (jnp.moveaxis(states, 1, 0), jnp.moveaxis(chunk_decay, 1, 0)))
    h_in = jnp.moveaxis(h_in, 0, 1)
    y_off = jnp.einsum("bcthn,bchpn,bcht->bcthp", cc, h_in, jnp.exp(acum))
    return (y_diag + y_off).reshape(bsz, seq, nh, hp)


def ssd_mixer(z, xbc, dt, conv_w, conv_b, dt_bias, a_log, d_skip, norm_g):
    bsz, seq, _ = z.shape
    f32 = jnp.float32
    xbc = jax.nn.silu(causal_dwconv(xbc, conv_w) + conv_b).astype(f32)
    gs = SSD_GROUPS * SSD_STATE
    rep = SSD_HEADS // SSD_GROUPS
    xs = xbc[..., :SSD_INNER].reshape(bsz, seq, SSD_HEADS, SSD_HEAD_DIM)
    bm = jnp.repeat(xbc[..., SSD_INNER:SSD_INNER + gs].reshape(bsz, seq, SSD_GROUPS, SSD_STATE), rep, axis=2)
    cm = jnp.repeat(xbc[..., SSD_INNER + gs:].reshape(bsz, seq, SSD_GROUPS, SSD_STATE), rep, axis=2)
    dt = jax.nn.softplus(dt.astype(f32) + dt_bias.astype(f32))
    a = -jnp.exp(a_log.astype(f32))
    y = ssd_scan(xs, dt, a, bm, cm) + xs * d_skip.astype(f32)[:, None]
    y = y.reshape(bsz, seq, SSD_INNER) * jax.nn.silu(z.astype(f32))
    return rms_norm(y, norm_g).astype(z.dtype)


def gated_delta_scan(q, k, v, beta, g):
    bsz, seq, nh, dk = q.shape
    dv = v.shape[-1]
    qc, kc, vc = to_chunks(q), to_chunks(k), to_chunks(v)
    bc, gcum = to_chunks(beta), jnp.cumsum(to_chunks(g), axis=-1)
    incl = jnp.tril(jnp.ones((CHUNK, CHUNK), bool))
    strict = jnp.tril(jnp.ones((CHUNK, CHUNK), bool), k=-1)
    decay = jnp.exp(jnp.where(incl, gcum[..., :, None] - gcum[..., None, :], -jnp.inf))
    kb = kc * bc[..., None]
    a_mat = jnp.where(strict, jnp.einsum("nbhid,nbhjd->nbhij", kb, kc) * decay, 0.0)
    eye = jnp.eye(CHUNK, dtype=jnp.float32)
    t_mat = lax.linalg.triangular_solve(eye + a_mat, jnp.broadcast_to(eye, a_mat.shape),
                                        left_side=True, lower=True)
    u = jnp.einsum("nbhij,nbhjd->nbhid", t_mat, vc * bc[..., None])
    w = jnp.einsum("nbhij,nbhjd->nbhid", t_mat, kb * jnp.exp(gcum)[..., None])
    qk = jnp.where(incl, jnp.einsum("nbhid,nbhjd->nbhij", qc, kc) * decay, 0.0)
    q_dec = qc * jnp.exp(gcum)[..., None]
    k_dec = kc * jnp.exp(gcum[..., -1:] - gcum)[..., None]
    last = jnp.exp(gcum[..., -1])

    def step(s, inp):
        u_c, w_c, qk_c, qd_c, kd_c, l_c = inp
        v_new = u_c - jnp.einsum("bhid,bhde->bhie", w_c, s)
        o = jnp.einsum("bhid,bhde->bhie", qd_c, s) + jnp.einsum("bhij,bhje->bhie", qk_c, v_new)
        s = s * l_c[..., None, None] + jnp.einsum("bhjd,bhje->bhde", kd_c, v_new)
        return s, o

    s0 = jnp.zeros((bsz, nh, dk, dv), jnp.float32)
    _, o = lax.scan(step, s0, (u, w, qk, q_dec, k_dec, last))
    return from_chunks(o)


def gdn_mixer(qkv, a, b, gate, conv_w, a_log, dt_bias, norm_g):
    bsz, seq, _ = qkv.shape
    f32 = jnp.float32
    out_dtype = qkv.dtype
    shp = (bsz, seq, GDN_HEADS, GDN_HEAD_DIM)
    qkv = jax.nn.silu(causal_dwconv(qkv, conv_w)).astype(f32)
    q = l2_norm(qkv[..., :GDN_INNER].reshape(shp)) * GDN_HEAD_DIM ** -0.5
    k = l2_norm(qkv[..., GDN_INNER:2 * GDN_INNER].reshape(shp))
    v = qkv[..., 2 * GDN_INNER:].reshape(shp)
    beta = jax.nn.sigmoid(b.astype(f32))
    g = -jnp.exp(a_log.astype(f32)) * jax.nn.softplus(a.astype(f32) + dt_bias.astype(f32))
    o = gated_delta_scan(q, k, v, beta, g)
    o = rms_norm(o, norm_g) * jax.nn.silu(gate.astype(f32).reshape(shp))
    return o.reshape(bsz, seq, GDN_INNER).astype(out_dtype)


def gla_scan(q, k, v, gk):
    bsz, seq, nh, dk = q.shape
    dv = v.shape[-1]
    gcum = jnp.cumsum(to_chunks(gk), axis=-2)
    incl = jnp.tril(jnp.ones((CHUNK, CHUNK), bool))[:, :, None]

    def step(s, inp):
        qc, kc, vc, gc = inp
        diff = gc[:, :, :, None, :] - gc[:, :, None, :, :]
        dec = jnp.exp(jnp.where(incl, diff, -jnp.inf))
        att = jnp.einsum("bhtd,bhsd,bhtsd->bhts", qc, kc, dec)
        o = jnp.einsum("bhts,bhse->bhte", att, vc) + jnp.einsum("bhtd,bhde->bhte", qc * jnp.exp(gc), s)
        s = s * jnp.exp(gc[:, :, -1, :])[..., None] + jnp.einsum(
            "bhsd,bhse->bhde", kc * jnp.exp(gc[:, :, -1:, :] - gc), vc)
        return s, o

    s0 = jnp.zeros((bsz, nh, dk, dv), jnp.float32)
    _, o = lax.scan(step, s0, (to_chunks(q), to_chunks(k), to_chunks(v), gcum))
    return from_chunks(o)


def gla_mixer(q, k, v, glr, r, w_gate2, b_gate, norm_g):
    bsz, seq, _ = q.shape
    f32 = jnp.float32
    kshape = (bsz, seq, GLA_HEADS, GLA_KEY_DIM)
    vshape = (bsz, seq, GLA_HEADS, GLA_VAL_DIM)
    gk = jax.nn.log_sigmoid((glr @ w_gate2 + b_gate).astype(f32)) / GLA_GATE_NORMALIZER
    o = gla_scan(q.astype(f32).reshape(kshape) * GLA_KEY_DIM ** -0.5, k.astype(f32).reshape(kshape),
                 v.astype(f32).reshape(vshape), gk.reshape(kshape))
    o = rms_norm(o, norm_g) * jax.nn.silu(r.astype(f32).reshape(vshape))
    return o.reshape(bsz, seq, GLA_VAL_WIDTH).astype(q.dtype)


def dsa_mixer(q, k, v, iq, ik, iw, q_norm, k_norm, ik_norm):
    bsz, seq, _ = q.shape
    q = rms_norm(q.reshape(bsz, seq, DSA_HEADS, DSA_HEAD_DIM), q_norm)
    k = rms_norm(k, k_norm)
    iq = iq.reshape(bsz, seq, IDX_HEADS, IDX_DIM)
    ik = rms_norm(ik, ik_norm)
    iw = iw * (IDX_HEADS ** -0.5 * IDX_DIM ** -0.5)
    topk = min(DSA_TOPK_MAX, seq // 4)
    key_chunk = jnp.arange(seq) // CHUNK
    gather = jax.vmap(lambda t, i: t[i])

    def block(i):
        start = i * DSA_QBLOCK
        qb = lax.dynamic_slice_in_dim(q, start, DSA_QBLOCK, axis=1)
        iqb = lax.dynamic_slice_in_dim(iq, start, DSA_QBLOCK, axis=1)
        iwb = lax.dynamic_slice_in_dim(iw, start, DSA_QBLOCK, axis=1)
        q_chunk = (start + jnp.arange(DSA_QBLOCK)) // CHUNK
        score = jnp.einsum("bths,bth->bts", jax.nn.relu(jnp.einsum("bthd,bsd->bths", iqb, ik)),
                           iwb).astype(jnp.float32)
        score = jnp.where(key_chunk[None, None, :] <= q_chunk[None, :, None], score, -jnp.inf)
        _, idx = lax.top_k(score, topk)
        valid = key_chunk[idx] <= q_chunk[None, :, None]
        kg = gather(k, idx)
        vg = gather(v, idx)
        s = jnp.einsum("bthd,btkd->bhtk", qb, kg).astype(jnp.float32) * DSA_HEAD_DIM ** -0.5
        s = jnp.where(valid[:, None], s, -jnp.inf)
        p = jax.nn.softmax(s, axis=-1).astype(v.dtype)
        return jnp.einsum("bhtk,btkd->bthd", p, vg)

    out = lax.map(block, jnp.arange(seq // DSA_QBLOCK))
    return jnp.moveaxis(out, 0, 1).reshape(bsz, seq, DSA_HEADS * DSA_HEAD_DIM)


def memory_xattn_core(q, m, wk, wv, q_norm, k_norm):
    bsz, seq, _ = q.shape
    nm = m.shape[1]
    q = rms_norm(q.reshape(bsz, seq, MEM_HEADS, MEM_HEAD_DIM), q_norm)
    k = rms_norm((m @ wk).reshape(bsz, nm, MEM_HEADS, MEM_HEAD_DIM), k_norm)
    v = (m @ wv).reshape(bsz, nm, MEM_HEADS, MEM_HEAD_DIM)
    s = jnp.einsum("bthd,bshd->bhts", q, k).astype(jnp.float32) * MEM_HEAD_DIM ** -0.5
    p = jax.nn.softmax(s, axis=-1).astype(v.dtype)
    return jnp.einsum("bhts,bshd->bthd", p, v).reshape(bsz, seq, MEM_WIDTH)


def peer_core(h, qall, sub_keys, u_tab, v_tab):
    bsz, seq, dm = h.shape
    half = PEER_QDIM // 2
    ncand = PEER_TOPK * PEER_TOPK

    def block(i):
        hb = lax.dynamic_slice_in_dim(h, i * PEER_BLOCK, PEER_BLOCK, axis=1)
        q = lax.dynamic_slice_in_dim(qall, i * PEER_BLOCK, PEER_BLOCK, axis=1)
        q = q.reshape(bsz, PEER_BLOCK, PEER_HEADS, 2, half)
        s = jnp.einsum("bthpd,hpnd->bthpn", q, sub_keys).astype(jnp.float32)
        s1, i1 = lax.top_k(s[..., 0, :], PEER_TOPK)
        s2, i2 = lax.top_k(s[..., 1, :], PEER_TOPK)
        cand = (s1[..., :, None] + s2[..., None, :]).reshape(bsz, PEER_BLOCK, PEER_HEADS, ncand)
        cidx = (i1[..., :, None] * PEER_NKEYS + i2[..., None, :]).reshape(bsz, PEER_BLOCK, PEER_HEADS, ncand)
        top, pos = lax.top_k(cand, PEER_TOPK)
        eidx = jnp.take_along_axis(cidx, pos, axis=-1)
        gate = jax.nn.softmax(top, axis=-1)
        act = jax.nn.gelu(jnp.einsum("bthkd,btd->bthk", u_tab[eidx], hb).astype(jnp.float32))
        coef = (gate * act).astype(hb.dtype)
        return jnp.einsum("bthk,bthkd->btd", coef, v_tab[eidx])

    out = lax.map(block, jnp.arange(seq // PEER_BLOCK))
    return jnp.moveaxis(out, 0, 1).reshape(bsz, seq, dm)


def kernel(x, mem, mix_norm, w_in, ssd_conv_w, ssd_conv_b, ssd_dt_bias, ssd_a_log, ssd_d, ssd_norm,
           gdn_conv_w, gdn_a_log, gdn_dt_bias, gdn_norm, gla_w_gate, gla_b_gate, gla_norm,
           dsa_q_norm, dsa_k_norm, idx_k_norm, w_out, xattn_norm, mem_norm, xattn_wq, xattn_wk,
           xattn_wv, xattn_wo, xattn_q_norm, xattn_k_norm, ffn_norm, peer_w_query, peer_sub_keys,
           peer_u, peer_v):
    bsz, seq, dm = x.shape
    depth = w_in.shape[0]
    splits = np.cumsum(np.array(IN_SIZES))[:-1].tolist()
    x2 = x.reshape(bsz * seq, dm)
    for l in range(depth):
        p = _matmul(x2, w_in[l], gain=mix_norm[l]).reshape(bsz, seq, IN_WIDTH)
        (ssd_z, ssd_xbc, ssd_dt, gdn_qkv, gdn_a, gdn_b, gdn_g, gla_q, gla_k, gla_v, gla_glr, gla_r,
         dsa_q, dsa_k, dsa_v, idx_q, idx_k, idx_w) = jnp.split(p, splits, axis=-1)
        y_a = ssd_mixer(ssd_z, ssd_xbc, ssd_dt, ssd_conv_w[l], ssd_conv_b[l], ssd_dt_bias[l],
                        ssd_a_log[l], ssd_d[l], ssd_norm[l])
        y_b = gdn_mixer(gdn_qkv, gdn_a, gdn_b, gdn_g, gdn_conv_w[l], gdn_a_log[l], gdn_dt_bias[l], gdn_norm[l])
        y_c = gla_mixer(gla_q, gla_k, gla_v, gla_glr, gla_r, gla_w_gate[l], gla_b_gate[l], gla_norm[l])
        y_d = dsa_mixer(dsa_q, dsa_k, dsa_v, idx_q, idx_k, idx_w, dsa_q_norm[l], dsa_k_norm[l], idx_k_norm[l])
        y = jnp.concatenate([y_a, y_b, y_c, y_d], axis=-1).reshape(bsz * seq, -1)
        x2 = _matmul(y, w_out[l], residual=x2)
        q = _matmul(x2, xattn_wq[l], gain=xattn_norm[l]).reshape(bsz, seq, MEM_WIDTH)
        o = memory_xattn_core(q, rms_norm(mem, mem_norm[l]), xattn_wk[l], xattn_wv[l],
                              xattn_q_norm[l], xattn_k_norm[l])
        x2 = _matmul(o.reshape(bsz * seq, MEM_WIDTH), xattn_wo[l], residual=x2)
        hq = _matmul(x2, peer_w_query[l], gain=ffn_norm[l])
        hn = rms_norm(x2, ffn_norm[l]).reshape(bsz, seq, dm)
        x2 = x2 + peer_core(hn, hq.reshape(bsz, seq, -1), peer_sub_keys[l], peer_u[l], peer_v[l]).reshape(bsz * seq, dm)
    return x2.reshape(bsz, seq, dm)
```

```python
import functools

import jax
import jax.numpy as jnp
import numpy as np
from jax import lax
from jax.experimental import pallas as pl
from jax.experimental.pallas import tpu as pltpu

CHUNK = 64
EPS = 1e-6
SSD_HEADS, SSD_HEAD_DIM, SSD_GROUPS, SSD_STATE, SSD_CONV = 4, 64, 2, 64, 4
SSD_INNER = SSD_HEADS * SSD_HEAD_DIM
SSD_CONV_DIM = SSD_INNER + 2 * SSD_GROUPS * SSD_STATE
GDN_HEADS, GDN_HEAD_DIM, GDN_CONV = 4, 64, 4
GDN_INNER = GDN_HEADS * GDN_HEAD_DIM
GLA_HEADS, GLA_KEY_DIM, GLA_VAL_DIM, GLA_GATE_RANK = 4, 32, 64, 16
GLA_KEY_WIDTH = GLA_HEADS * GLA_KEY_DIM
GLA_VAL_WIDTH = GLA_HEADS * GLA_VAL_DIM
GLA_GATE_NORMALIZER = 16.0
DSA_HEADS, DSA_HEAD_DIM, IDX_HEADS, IDX_DIM = 4, 64, 8, 32
DSA_TOPK_MAX, DSA_QBLOCK = 256, 128
MEM_HEADS, MEM_HEAD_DIM = 4, 64
MEM_WIDTH = MEM_HEADS * MEM_HEAD_DIM
PEER_HEADS, PEER_NKEYS, PEER_QDIM, PEER_TOPK = 8, 128, 128, 16
PEER_HALF = PEER_QDIM // 2
PEER_SLOTS = PEER_HEADS * PEER_TOPK

IN_SIZES = (
    SSD_INNER, SSD_CONV_DIM, SSD_HEADS,
    3 * GDN_INNER, GDN_HEADS, GDN_HEADS, GDN_INNER,
    GLA_KEY_WIDTH, GLA_KEY_WIDTH, GLA_VAL_WIDTH, GLA_GATE_RANK, GLA_VAL_WIDTH,
    DSA_HEADS * DSA_HEAD_DIM, DSA_HEAD_DIM, DSA_HEAD_DIM,
    IDX_HEADS * IDX_DIM, IDX_DIM, IDX_HEADS,
)
IN_WIDTH = sum(IN_SIZES)

LANE, SUBLANE = 128, 8
VMEM_LIMIT = 48 * 1024 * 1024
INT_MIN = -2 ** 31


def _round_up(n, m):
    return (n + m - 1) // m * m


def _mm_kernel(x_ref, g_ref, w_ref, *rest, normalize, has_res):
    if has_res:
        r_ref, o_ref, xs_ref = rest
    else:
        o_ref, xs_ref = rest

    @pl.when(pl.program_id(1) == 0)
    def _():
        xf = x_ref[...]
        if normalize:
            xf = xf * lax.rsqrt(jnp.mean(xf * xf, axis=-1, keepdims=True) + EPS) * g_ref[...]
        xs_ref[...] = xf.astype(jnp.bfloat16)

    acc = jnp.dot(xs_ref[...], w_ref[...], preferred_element_type=jnp.float32)
    if has_res:
        acc = acc + r_ref[...]
    o_ref[...] = acc


def _matmul(x, w, gain=None, residual=None, tm=512, tn=256):
    m, k = x.shape
    n = w.shape[1]
    n_pad = _round_up(n, tn)
    wb = w.astype(jnp.bfloat16)
    if n_pad != n:
        wb = jnp.pad(wb, ((0, 0), (0, n_pad - n)))
    normalize = gain is not None
    g = (gain if normalize else jnp.ones((k,), jnp.float32)).reshape(1, k).astype(jnp.float32)
    has_res = residual is not None
    assert m % tm == 0
    in_specs = [
        pl.BlockSpec((tm, k), lambda i, j: (i, 0)),
        pl.BlockSpec((1, k), lambda i, j: (0, 0)),
        pl.BlockSpec((k, tn), lambda i, j: (0, j)),
    ]
    args = [x, g, wb]
    if has_res:
        assert n_pad == n
        in_specs.append(pl.BlockSpec((tm, tn), lambda i, j: (i, j)))
        args.append(residual)
    out = pl.pallas_call(
        functools.partial(_mm_kernel, normalize=normalize, has_res=has_res),
        grid=(m // tm, n_pad // tn),
        in_specs=in_specs,
        out_specs=pl.BlockSpec((tm, tn), lambda i, j: (i, j)),
        out_shape=jax.ShapeDtypeStruct((m, n_pad), jnp.float32),
        scratch_shapes=[pltpu.VMEM((tm, k), jnp.bfloat16)],
        compiler_params=pltpu.CompilerParams(
            dimension_semantics=("arbitrary", "arbitrary"), vmem_limit_bytes=VMEM_LIMIT),
        name="norm_matmul",
    )(*args)
    return out[:, :n] if n_pad != n else out


def _dsa_kernel(qT_ref, k_ref, vT_ref, iqT_ref, ik_ref, iwT_ref, qg_ref, kg_ref, ikg_ref, o_ref,
                qn_s, kn_s, ikn_s, iq_s, vT_s, key_s, *, seq, topk):
    f32, bf16, i32 = jnp.float32, jnp.bfloat16, jnp.int32
    nq = DSA_QBLOCK
    for h in range(DSA_HEADS):
        rows = slice(h * DSA_HEAD_DIM, (h + 1) * DSA_HEAD_DIM)
        qh = qT_ref[rows, :]
        ms = jnp.mean(qh * qh, axis=0, keepdims=True)
        qn_s[rows, :] = (qh * lax.rsqrt(ms + EPS) * qg_ref[...]).astype(bf16)
    kk = k_ref[...]
    kn_s[...] = (kk * lax.rsqrt(jnp.mean(kk * kk, axis=-1, keepdims=True) + EPS) * kg_ref[...]).astype(bf16)
    ikk = ik_ref[...]
    ikn_s[...] = (ikk * lax.rsqrt(jnp.mean(ikk * ikk, axis=-1, keepdims=True) + EPS) * ikg_ref[...]).astype(bf16)
    iq_s[...] = iqT_ref[...].astype(bf16)
    vT_s[...] = vT_ref[...].astype(bf16)
    iw_scale = IDX_HEADS ** -0.5 * IDX_DIM ** -0.5
    n_jbits = max(1, int(seq).bit_length())

    def count(mask):
        return jnp.sum(mask.astype(i32), axis=0, keepdims=True)

    for i in range(seq // nq):
        nk = nq * (i + 1)
        qs = slice(i * nq, (i + 1) * nq)
        kidx = lax.broadcasted_iota(i32, (nk, nq), 0)
        lane = lax.broadcasted_iota(i32, (nk, nq), 1)
        valid = (kidx < i * nq + CHUNK) | (lane >= CHUNK)
        if nk > topk:
            sc = jnp.zeros((nk, nq), f32)
            for h in range(IDX_HEADS):
                a = jnp.dot(ikn_s[:nk, :], iq_s[h * IDX_DIM:(h + 1) * IDX_DIM, qs], preferred_element_type=f32)
                sc = sc + jnp.maximum(a, 0.0) * (iwT_ref[h:h + 1, qs] * iw_scale)
            sc = jnp.where(sc == 0.0, 0.0, sc)
            bits = pltpu.bitcast(sc, i32)
            key = jnp.where(bits < 0, bits ^ 0x7FFFFFFF, bits)
            key_s[:nk, :] = jnp.where(valid, key, INT_MIN)

            t0 = jnp.where(count(key_s[:nk, :] >= 0) >= topk, 0, INT_MIN).astype(i32)

            def tbody(b, t):
                cand = t + jnp.left_shift(jnp.int32(1), 30 - b)
                return jnp.where(count(key_s[:nk, :] >= cand) >= topk, cand, t)

            thr = lax.fori_loop(0, 31, tbody, t0)
            keyv = key_s[:nk, :]
            gt = keyv > thr
            need = topk - count(gt)

            def jbody(b, j):
                cand = j + jnp.left_shift(jnp.int32(1), n_jbits - 1 - b)
                f = count((key_s[:nk, :] == thr) & (kidx < cand))
                return jnp.where(f <= need, cand, j)

            jlim = lax.fori_loop(0, n_jbits, jbody, jnp.zeros((1, nq), i32))
            sel = gt | ((keyv == thr) & (kidx < jlim))
        else:
            sel = valid
        for h in range(DSA_HEADS):
            rows = slice(h * DSA_HEAD_DIM, (h + 1) * DSA_HEAD_DIM)
            s = jnp.dot(kn_s[:nk, :], qn_s[rows, qs], preferred_element_type=f32) * DSA_HEAD_DIM ** -0.5
            s = jnp.where(sel, s, -jnp.inf)
            m = jnp.max(s, axis=0, keepdims=True)
            p = jnp.exp(s - m)
            l = jnp.sum(p, axis=0, keepdims=True)
            oT = jnp.dot(vT_s[:, :nk], p.astype(bf16), preferred_element_type=f32)
            o_ref[rows, qs] = oT / l


def dsa_attention(qT, k, vT, iqT, ik, iwT, q_norm, k_norm, ik_norm):
    bsz, _, seq = qT.shape
    assert seq % DSA_QBLOCK == 0
    topk = min(DSA_TOPK_MAX, seq // 4)
    hd, idd = DSA_HEAD_DIM, IDX_DIM
    bspec = lambda *shape: pl.BlockSpec((None,) + shape, lambda b: (b,) + (0,) * len(shape))
    cspec = lambda *shape: pl.BlockSpec(shape, lambda b: (0,) * len(shape))
    bf16 = jnp.bfloat16
    return pl.pallas_call(
        functools.partial(_dsa_kernel, seq=seq, topk=topk),
        grid=(bsz,),
        in_specs=[bspec(DSA_HEADS * hd, seq), bspec(seq, hd), bspec(hd, seq), bspec(IDX_HEADS * idd, seq),
                  bspec(seq, idd), bspec(IDX_HEADS, seq), cspec(hd, 1), cspec(1, hd), cspec(1, idd)],
        out_specs=bspec(DSA_HEADS * hd, seq),
        out_shape=jax.ShapeDtypeStruct((bsz, DSA_HEADS * hd, seq), jnp.float32),
        scratch_shapes=[pltpu.VMEM((DSA_HEADS * hd, seq), bf16), pltpu.VMEM((seq, hd), bf16),
                        pltpu.VMEM((seq, idd), bf16), pltpu.VMEM((IDX_HEADS * idd, seq), bf16),
                        pltpu.VMEM((hd, seq), bf16), pltpu.VMEM((seq, DSA_QBLOCK), jnp.int32)],
        compiler_params=pltpu.CompilerParams(dimension_semantics=("arbitrary",), vmem_limit_bytes=VMEM_LIMIT),
        name="dsa_attention",
    )(qT, k, vT, iqT, ik, iwT, q_norm.reshape(hd, 1), k_norm.reshape(1, hd), ik_norm.reshape(1, idd))


def dsa_mixer(q, k, v, iq, ik, iw, q_norm, k_norm, ik_norm):
    t = lambda a: jnp.swapaxes(a, 1, 2)
    return t(dsa_attention(t(q), k, t(v), t(iq), ik, t(iw), q_norm, k_norm, ik_norm))


def _top16(vals, payload=None):
    n = vals.shape[0]
    iota = lax.broadcasted_iota(jnp.int32, vals.shape, 0)
    tv, ti = [], []
    for _ in range(PEER_TOPK):
        m = jnp.max(vals, axis=0, keepdims=True)
        idx = jnp.min(jnp.where(vals == m, iota, n), axis=0, keepdims=True)
        hit = iota == idx
        tv.append(m)
        ti.append(idx if payload is None else jnp.max(jnp.where(hit, payload, -1), axis=0, keepdims=True))
        vals = jnp.where(hit, -jnp.inf, vals)
    return jnp.concatenate(tv, axis=0), jnp.concatenate(ti, axis=0)


def _peer_select_kernel(x_ref, g_ref, wqT_ref, sk_ref, hn_ref, eidx_ref, gate_ref, qT_s, *, tb):
    f32, bf16 = jnp.float32, jnp.bfloat16
    x = x_ref[...]
    hn = x * lax.rsqrt(jnp.mean(x * x, axis=-1, keepdims=True) + EPS) * g_ref[...]
    hn_ref[...] = hn
    qT_s[...] = lax.dot_general(wqT_ref[...], hn.astype(bf16), (((1,), (1,)), ((), ())),
                                preferred_element_type=f32)

    def head(h, carry):
        for ct in range(tb // LANE):
            cols = slice(ct * LANE, (ct + 1) * LANE)
            tops = []
            for p in range(2):
                hp = h * 2 + p
                qhp = qT_s[pl.ds(pl.multiple_of(hp * PEER_HALF, PEER_HALF), PEER_HALF), cols]
                s = jnp.dot(sk_ref[hp], qhp.astype(bf16), preferred_element_type=f32)
                tops.append(_top16(s))
            (s1, i1), (s2, i2) = tops
            cand = jnp.concatenate([s1[i:i + 1] + s2 for i in range(PEER_TOPK)], axis=0)
            cidx = jnp.concatenate([i1[i:i + 1] * PEER_NKEYS + i2 for i in range(PEER_TOPK)], axis=0)
            top, eidx = _top16(cand, cidx)
            g = jnp.exp(top - top[0:1])
            g = g / jnp.sum(g, axis=0, keepdims=True)
            rows = pl.ds(pl.multiple_of(h * PEER_TOPK, PEER_TOPK), PEER_TOPK)
            eidx_ref[rows, cols] = eidx
            gate_ref[rows, cols] = g
        return carry

    lax.fori_loop(0, PEER_HEADS, head, 0)


def peer_select(x2, gain, w_query, sub_keys, tb=256):
    n, dm = x2.shape
    nq = w_query.shape[1]
    wqT = w_query.T.astype(jnp.bfloat16)
    sk = sub_keys.reshape(PEER_HEADS * 2, PEER_NKEYS, PEER_HALF).astype(jnp.bfloat16)
    return pl.pallas_call(
        functools.partial(_peer_select_kernel, tb=tb),
        grid=(n // tb,),
        in_specs=[pl.BlockSpec((tb, dm), lambda i: (i, 0)), pl.BlockSpec((1, dm), lambda i: (0, 0)),
                  pl.BlockSpec((nq, dm), lambda i: (0, 0)),
                  pl.BlockSpec((PEER_HEADS * 2, PEER_NKEYS, PEER_HALF), lambda i: (0, 0, 0))],
        out_specs=[pl.BlockSpec((tb, dm), lambda i: (i, 0)), pl.BlockSpec((PEER_SLOTS, tb), lambda i: (0, i)),
                   pl.BlockSpec((PEER_SLOTS, tb), lambda i: (0, i))],
        out_shape=[jax.ShapeDtypeStruct((n, dm), jnp.float32), jax.ShapeDtypeStruct((PEER_SLOTS, n), jnp.int32),
                   jax.ShapeDtypeStruct((PEER_SLOTS, n), jnp.float32)],
        scratch_shapes=[pltpu.VMEM((nq, tb), jnp.float32)],
        compiler_params=pltpu.CompilerParams(dimension_semantics=("arbitrary",), vmem_limit_bytes=VMEM_LIMIT),
        name="peer_select",
    )(x2, gain.reshape(1, dm), wqT, sk)


def pack_table(tab):
    e, dm = tab.shape
    b = lax.bitcast_convert_type(tab.astype(jnp.bfloat16), jnp.uint16).astype(jnp.uint32)
    packed = (b[0::2] << 16) | b[1::2]
    return lax.bitcast_convert_type(packed, jnp.int32).reshape(e // 2 * (dm // LANE), LANE)


def _gather_row(tab_ref, e):
    w = tab_ref[pl.ds(pl.multiple_of((e >> 1) * SUBLANE, SUBLANE), SUBLANE), :]
    return pltpu.bitcast(jnp.left_shift(w, (e & 1) * 16) & jnp.int32(-65536), jnp.float32)


def _sublane_sums(tiles):
    sub = lax.broadcasted_iota(jnp.int32, (SUBLANE, LANE), 0)

    def merge(p, q, shift):
        keep = (sub % (2 * shift)) < shift
        return jnp.where(keep, p, pltpu.roll(q, shift, 0)) + jnp.where(keep, pltpu.roll(p, SUBLANE - shift, 0), q)

    a = tiles
    b = [merge(a[0], a[4], 4), merge(a[2], a[6], 4), merge(a[1], a[5], 4), merge(a[3], a[7], 4)]
    c = [merge(b[0], b[1], 2), merge(b[2], b[3], 2)]
    return merge(c[0], c[1], 1)


def _peer_act_kernel(eidx_ref, h_ref, gate_ref, tab_ref, coef_ref, r_s, act_s, *, tb):
    f32 = jnp.float32
    ones = jnp.ones((SUBLANE, LANE), f32)

    def token(t, carry):
        ht = h_ref[pl.ds(pl.multiple_of(t * SUBLANE, SUBLANE), SUBLANE), :]
        for g in range(PEER_SLOTS // SUBLANE):
            tiles = [_gather_row(tab_ref, eidx_ref[t * PEER_SLOTS + g * SUBLANE + r]) * ht for r in range(SUBLANE)]
            r_s[g * SUBLANE:(g + 1) * SUBLANE, :] = _sublane_sums(tiles)
        act = lax.dot_general(ones, r_s[...], (((1,), (1,)), ((), ())), precision=lax.Precision.HIGHEST,
                              preferred_element_type=f32)
        act_s[pl.ds(t, 1), :] = act[0:1, :]
        return carry

    lax.fori_loop(0, tb, token, 0)
    coef_ref[...] = gate_ref[...] * jax.nn.gelu(act_s[...])


def peer_act(eidx, hn, gate, tab_u, tb=128):
    n, dm = hn.shape
    rows = dm // LANE
    return pl.pallas_call(
        functools.partial(_peer_act_kernel, tb=tb),
        grid=(n // tb,),
        in_specs=[pl.BlockSpec((tb * PEER_SLOTS,), lambda i: (i,), memory_space=pltpu.SMEM),
                  pl.BlockSpec((tb * rows, LANE), lambda i: (i, 0)),
                  pl.BlockSpec((tb, PEER_SLOTS), lambda i: (i, 0)),
                  pl.BlockSpec(tab_u.shape, lambda i: (0, 0), pipeline_mode=pl.Buffered(1))],
        out_specs=pl.BlockSpec((tb, PEER_SLOTS), lambda i: (i, 0)),
        out_shape=jax.ShapeDtypeStruct((n, PEER_SLOTS), jnp.float32),
        scratch_shapes=[pltpu.VMEM((PEER_SLOTS, LANE), jnp.float32), pltpu.VMEM((tb, PEER_SLOTS), jnp.float32)],
        compiler_params=pltpu.CompilerParams(dimension_semantics=("arbitrary",), vmem_limit_bytes=VMEM_LIMIT),
        name="peer_act",
    )(eidx, hn.reshape(n * rows, LANE), gate, tab_u)


def _peer_out_kernel(eidx_ref, coef_ref, x_ref, tab_ref, o_ref, *, tb):
    nacc = 4

    def token(t, carry):
        accs = [jnp.zeros((SUBLANE, LANE), jnp.float32) for _ in range(nacc)]
        for s in range(PEER_SLOTS):
            j = t * PEER_SLOTS + s
            accs[s % nacc] = accs[s % nacc] + coef_ref[j] * _gather_row(tab_ref, eidx_ref[j])
        rows = pl.ds(pl.multiple_of(t * SUBLANE, SUBLANE), SUBLANE)
        o_ref[rows, :] = x_ref[rows, :] + ((accs[0] + accs[1]) + (accs[2] + accs[3]))
        return carry

    lax.fori_loop(0, tb, token, 0)


def peer_out(eidx, coef, x2, tab_v, tb=128):
    n, dm = x2.shape
    rows = dm // LANE
    out = pl.pallas_call(
        functools.partial(_peer_out_kernel, tb=tb),
        grid=(n // tb,),
        in_specs=[pl.BlockSpec((tb * PEER_SLOTS,), lambda i: (i,), memory_space=pltpu.SMEM),
                  pl.BlockSpec((tb * PEER_SLOTS,), lambda i: (i,), memory_space=pltpu.SMEM),
                  pl.BlockSpec((tb * rows, LANE), lambda i: (i, 0)),
                  pl.BlockSpec(tab_v.shape, lambda i: (0, 0), pipeline_mode=pl.Buffered(1))],
        out_specs=pl.BlockSpec((tb * rows, LANE), lambda i: (i, 0)),
        out_shape=jax.ShapeDtypeStruct((n * rows, LANE), jnp.float32),
        compiler_params=pltpu.CompilerParams(dimension_semantics=("arbitrary",), vmem_limit_bytes=VMEM_LIMIT),
        name="peer_out",
    )(eidx, coef, x2.reshape(n * rows, LANE), tab_v)
    return out.reshape(n, dm)


def peer_layer(x2, gain, w_query, sub_keys, tab_u, tab_v):
    n = x2.shape[0]
    hn, eidxT, gateT = peer_select(x2, gain, w_query, sub_keys)
    eidx = eidxT.T.reshape(n * PEER_SLOTS)
    coef = peer_act(eidx, hn, gateT.T, tab_u)
    return peer_out(eidx, coef.reshape(n * PEER_SLOTS), x2, tab_v)


def rms_norm(x, g):
    xf = x.astype(jnp.float32)
    y = xf * lax.rsqrt(jnp.mean(xf * xf, axis=-1, keepdims=True) + EPS)
    return (y * g.astype(jnp.float32)).astype(x.dtype)


def l2_norm(x):
    xf = x.astype(jnp.float32)
    return xf * lax.rsqrt(jnp.sum(xf * xf, axis=-1, keepdims=True) + EPS)


def causal_dwconv(x, w):
    k = w.shape[0]
    return lax.conv_general_dilated(
        x, w[:, None, :].astype(x.dtype), window_strides=(1,), padding=[(k - 1, 0)],
        dimension_numbers=("NWC", "WIO", "NWC"), feature_group_count=x.shape[-1])


def to_chunks(t):
    bsz, seq, nh = t.shape[:3]
    t = t.reshape((bsz, seq // CHUNK, CHUNK, nh) + t.shape[3:])
    return jnp.moveaxis(jnp.moveaxis(t, 3, 2), 1, 0)


def from_chunks(t):
    nc, bsz, nh, q = t.shape[:4]
    t = jnp.moveaxis(jnp.moveaxis(t, 0, 1), 2, 3)
    return t.reshape((bsz, nc * q, nh) + t.shape[4:])


def ssd_scan(xs, dt, a, bm, cm):
    bsz, seq, nh, hp = xs.shape
    nc = seq // CHUNK
    xc = xs.reshape(bsz, nc, CHUNK, nh, hp)
    bc = bm.reshape(bsz, nc, CHUNK, nh, -1)
    cc = cm.reshape(bsz, nc, CHUNK, nh, -1)
    dtc = jnp.moveaxis(dt.reshape(bsz, nc, CHUNK, nh), 2, 3)
    acum = jnp.cumsum(dtc * a[:, None], axis=-1)
    causal = jnp.tril(jnp.ones((CHUNK, CHUNK), bool))
    seg = jnp.exp(jnp.where(causal, acum[..., :, None] - acum[..., None, :], -jnp.inf))
    cb = jnp.einsum("bcthn,bcshn->bchts", cc, bc)
    y_diag = jnp.einsum("bchts,bcshp->bcthp", cb * seg * dtc[..., None, :], xc)
    decay_end = jnp.exp(acum[..., -1:] - acum) * dtc
    states = jnp.einsum("bcshn,bchs,bcshp->bchpn", bc, decay_end, xc)
    chunk_decay = jnp.exp(acum[..., -1])

    def step(h, inp):
        st, dec = inp
        return h * dec[..., None, None] + st, h

    h0 = jnp.zeros((bsz, nh, hp, bc.shape[-1]), states.dtype)
    _, h_in = lax.scan(step, h0, (jnp.moveaxis(states, 1, 0), jnp.moveaxis(chunk_decay, 1, 0)))
    h_in = jnp.moveaxis(h_in, 0, 1)
    y_off = jnp.einsum("bcthn,bchpn,bcht->bcthp", cc, h_in, jnp.exp(acum))
    return (y_diag + y_off).reshape(bsz, seq, nh, hp)


def ssd_mixer(z, xbc, dt, conv_w, conv_b, dt_bias, a_log, d_skip, norm_g):
    bsz, seq, _ = z.shape
    f32 = jnp.float32
    xbc = jax.nn.silu(causal_dwconv(xbc, conv_w) + conv_b).astype(f32)
    gs = SSD_GROUPS * SSD_STATE
    rep = SSD_HEADS // SSD_GROUPS
    xs = xbc[..., :SSD_INNER].reshape(bsz, seq, SSD_HEADS, SSD_HEAD_DIM)
    bm = jnp.repeat(xbc[..., SSD_INNER:SSD_INNER + gs].reshape(bsz, seq, SSD_GROUPS, SSD_STATE), rep, axis=2)
    cm = jnp.repeat(xbc[..., SSD_INNER + gs:].reshape(bsz, seq, SSD_GROUPS, SSD_STATE), rep, axis=2)
    dt = jax.nn.softplus(dt.astype(f32) + dt_bias.astype(f32))
    a = -jnp.exp(a_log.astype(f32))
    y = ssd_scan(xs, dt, a, bm, cm) + xs * d_skip.astype(f32)[:, None]
    y = y.reshape(bsz, seq, SSD_INNER) * jax.nn.silu(z.astype(f32))
    return rms_norm(y, norm_g).astype(z.dtype)


def gated_delta_scan(q, k, v, beta, g):
    bsz, seq, nh, dk = q.shape
    dv = v.shape[-1]
    qc, kc, vc = to_chunks(q), to_chunks(k), to_chunks(v)
    bc, gcum = to_chunks(beta), jnp.cumsum(to_chunks(g), axis=-1)
    incl = jnp.tril(jnp.ones((CHUNK, CHUNK), bool))
    strict = jnp.tril(jnp.ones((CHUNK, CHUNK), bool), k=-1)
    decay = jnp.exp(jnp.where(incl, gcum[..., :, None] - gcum[..., None, :], -jnp.inf))
    kb = kc * bc[..., None]
    a_mat = jnp.where(strict, jnp.einsum("nbhid,nbhjd->nbhij", kb, kc) * decay, 0.0)
    eye = jnp.eye(CHUNK, dtype=jnp.float32)
    t_mat = lax.linalg.triangular_solve(eye + a_mat, jnp.broadcast_to(eye, a_mat.shape),
                                        left_side=True, lower=True)
    u = jnp.einsum("nbhij,nbhjd->nbhid", t_mat, vc * bc[..., None])
    w = jnp.einsum("nbhij,nbhjd->nbhid", t_mat, kb * jnp.exp(gcum)[..., None])
    qk = jnp.where(incl, jnp.einsum("nbhid,nbhjd->nbhij", qc, kc) * decay, 0.0)
    q_dec = qc * jnp.exp(gcum)[..., None]
    k_dec = kc * jnp.exp(gcum[..., -1:] - gcum)[..., None]
    last = jnp.exp(gcum[..., -1])

    def step(s, inp):
        u_c, w_c, qk_c, qd_c, kd_c, l_c = inp
        v_new = u_c - jnp.einsum("bhid,bhde->bhie", w_c, s)
        o = jnp.einsum("bhid,bhde->bhie", qd_c, s) + jnp.einsum("bhij,bhje->bhie", qk_c, v_new)
        s = s * l_c[..., None, None] + jnp.einsum("bhjd,bhje->bhde", kd_c, v_new)
        return s, o

    s0 = jnp.zeros((bsz, nh, dk, dv), jnp.float32)
    _, o = lax.scan(step, s0, (u, w, qk, q_dec, k_dec, last))
    return from_chunks(o)


def gdn_mixer(qkv, a, b, gate, conv_w, a_log, dt_bias, norm_g):
    bsz, seq, _ = qkv.shape
    f32 = jnp.float32
    out_dtype = qkv.dtype
    shp = (bsz, seq, GDN_HEADS, GDN_HEAD_DIM)
    qkv = jax.nn.silu(causal_dwconv(qkv, conv_w)).astype(f32)
    q = l2_norm(qkv[..., :GDN_INNER].reshape(shp)) * GDN_HEAD_DIM ** -0.5
    k = l2_norm(qkv[..., GDN_INNER:2 * GDN_INNER].reshape(shp))
    v = qkv[..., 2 * GDN_INNER:].reshape(shp)
    beta = jax.nn.sigmoid(b.astype(f32))
    g = -jnp.exp(a_log.astype(f32)) * jax.nn.softplus(a.astype(f32) + dt_bias.astype(f32))
    o = gated_delta_scan(q, k, v, beta, g)
    o = rms_norm(o, norm_g) * jax.nn.silu(gate.astype(f32).reshape(shp))
    return o.reshape(bsz, seq, GDN_INNER).astype(out_dtype)


def gla_scan(q, k, v, gk):
    bsz, seq, nh, dk = q.shape
    dv = v.shape[-1]
    gcum = jnp.cumsum(to_chunks(gk), axis=-2)
    incl = jnp.tril(jnp.ones((CHUNK, CHUNK), bool))[:, :, None]

    def step(s, inp):
        qc, kc, vc, gc = inp
        diff = gc[:, :, :, None, :] - gc[:, :, None, :, :]
        dec = jnp.exp(jnp.where(incl, diff, -jnp.inf))
        att = jnp.einsum("bhtd,bhsd,bhtsd->bhts", qc, kc, dec)
        o = jnp.einsum("bhts,bhse->bhte", att, vc) + jnp.einsum("bhtd,bhde->bhte", qc * jnp.exp(gc), s)
        s = s * jnp.exp(gc[:, :, -1, :])[..., None] + jnp.einsum(
            "bhsd,bhse->bhde", kc * jnp.exp(gc[:, :, -1:, :] - gc), vc)
        return s, o

    s0 = jnp.zeros((bsz, nh, dk, dv), jnp.float32)
    _, o = lax.scan(step, s0, (to_chunks(q), to_chunks(k), to_chunks(v), gcum))
    return from_chunks(o)


def gla_mixer(q, k, v, glr, r, w_gate2, b_gate, norm_g):
    bsz, seq, _ = q.shape
    f32 = jnp.float32
    kshape = (bsz, seq, GLA_HEADS, GLA_KEY_DIM)
    vshape = (bsz, seq, GLA_HEADS, GLA_VAL_DIM)
    gk = jax.nn.log_sigmoid((glr @ w_gate2 + b_gate).astype(f32)) / GLA_GATE_NORMALIZER
    o = gla_scan(q.astype(f32).reshape(kshape) * GLA_KEY_DIM ** -0.5, k.astype(f32).reshape(kshape),
                 v.astype(f32).reshape(vshape), gk.reshape(kshape))
    o = rms_norm(o, norm_g) * jax.nn.silu(r.astype(f32).reshape(vshape))
    return o.reshape(bsz, seq, GLA_VAL_WIDTH).astype(q.dtype)


def memory_xattn_core(q, m, wk, wv, q_norm, k_norm):
    bsz, seq, _ = q.shape
    nm = m.shape[1]
    q = rms_norm(q.reshape(bsz, seq, MEM_HEADS, MEM_HEAD_DIM), q_norm)
    k = rms_norm((m @ wk).reshape(bsz, nm, MEM_HEADS, MEM_HEAD_DIM), k_norm)
    v = (m @ wv).reshape(bsz, nm, MEM_HEADS, MEM_HEAD_DIM)
    s = jnp.einsum("bthd,bshd->bhts", q, k).astype(jnp.float32) * MEM_HEAD_DIM ** -0.5
    p = jax.nn.softmax(s, axis=-1).astype(v.dtype)
    return jnp.einsum("bhts,bshd->bthd", p, v).reshape(bsz, seq, MEM_WIDTH)


def kernel(x, mem, mix_norm, w_in, ssd_conv_w, ssd_conv_b, ssd_dt_bias, ssd_a_log, ssd_d, ssd_norm,
           gdn_conv_w, gdn_a_log, gdn_dt_bias, gdn_norm, gla_w_gate, gla_b_gate, gla_norm,
           dsa_q_norm, dsa_k_norm, idx_k_norm, w_out, xattn_norm, mem_norm, xattn_wq, xattn_wk,
           xattn_wv, xattn_wo, xattn_q_norm, xattn_k_norm, ffn_norm, peer_w_query, peer_sub_keys,
           peer_u, peer_v):
    bsz, seq, dm = x.shape
    depth = w_in.shape[0]
    splits = np.cumsum(np.array(IN_SIZES))[:-1].tolist()
    x2 = x.reshape(bsz * seq, dm)
    for l in range(depth):
        p = _matmul(x2, w_in[l], gain=mix_norm[l]).reshape(bsz, seq, IN_WIDTH)
        (ssd_z, ssd_xbc, ssd_dt, gdn_qkv, gdn_a, gdn_b, gdn_g, gla_q, gla_k, gla_v, gla_glr, gla_r,
         dsa_q, dsa_k, dsa_v, idx_q, idx_k, idx_w) = jnp.split(p, splits, axis=-1)
        y_a = ssd_mixer(ssd_z, ssd_xbc, ssd_dt, ssd_conv_w[l], ssd_conv_b[l], ssd_dt_bias[l],
                        ssd_a_log[l], ssd_d[l], ssd_norm[l])
        y_b = gdn_mixer(gdn_qkv, gdn_a, gdn_b, gdn_g, gdn_conv_w[l], gdn_a_log[l], gdn_dt_bias[l], gdn_norm[l])
        y_c = gla_mixer(gla_q, gla_k, gla_v, gla_glr, gla_r, gla_w_gate[l], gla_b_gate[l], gla_norm[l])
        y_d = dsa_mixer(dsa_q, dsa_k, dsa_v, idx_q, idx_k, idx_w, dsa_q_norm[l], dsa_k_norm[l], idx_k_norm[l])
        y = jnp.concatenate([y_a, y_b, y_c, y_d], axis=-1).reshape(bsz * seq, -1)
        x2 = _matmul(y, w_out[l], residual=x2)
        q = _matmul(x2, xattn_wq[l], gain=xattn_norm[l]).reshape(bsz, seq, MEM_WIDTH)
        o = memory_xattn_core(q, rms_norm(mem, mem_norm[l]), xattn_wk[l], xattn_wv[l],
                              xattn_q_norm[l], xattn_k_norm[l])
        x2 = _matmul(o.reshape(bsz * seq, MEM_WIDTH), xattn_wo[l], residual=x2)
        x2 = peer_layer(x2, ffn_norm[l], peer_w_query[l], peer_sub_keys[l],
                        pack_table(peer_u[l]), pack_table(peer_v[l]))
    return x2.reshape(bsz, seq, dm)
```

```python
import functools

import jax
import jax.numpy as jnp
import numpy as np
from jax import lax
from jax.experimental import pallas as pl
from jax.experimental.pallas import tpu as pltpu

CHUNK = 64
EPS = 1e-6
SSD_HEADS, SSD_HEAD_DIM, SSD_GROUPS, SSD_STATE, SSD_CONV = 4, 64, 2, 64, 4
SSD_INNER = SSD_HEADS * SSD_HEAD_DIM
SSD_CONV_DIM = SSD_INNER + 2 * SSD_GROUPS * SSD_STATE
GDN_HEADS, GDN_HEAD_DIM, GDN_CONV = 4, 64, 4
GDN_INNER = GDN_HEADS * GDN_HEAD_DIM
GLA_HEADS, GLA_KEY_DIM, GLA_VAL_DIM, GLA_GATE_RANK = 4, 32, 64, 16
GLA_KEY_WIDTH = GLA_HEADS * GLA_KEY_DIM
GLA_VAL_WIDTH = GLA_HEADS * GLA_VAL_DIM
GLA_GATE_NORMALIZER = 16.0
DSA_HEADS, DSA_HEAD_DIM, IDX_HEADS, IDX_DIM = 4, 64, 8, 32
DSA_TOPK_MAX, DSA_QBLOCK = 256, 128
MEM_HEADS, MEM_HEAD_DIM = 4, 64
MEM_WIDTH = MEM_HEADS * MEM_HEAD_DIM
PEER_HEADS, PEER_NKEYS, PEER_QDIM, PEER_TOPK = 8, 128, 128, 16
PEER_HALF = PEER_QDIM // 2
PEER_SLOTS = PEER_HEADS * PEER_TOPK

IN_SIZES = (
    SSD_INNER, SSD_CONV_DIM, SSD_HEADS,
    3 * GDN_INNER, GDN_HEADS, GDN_HEADS, GDN_INNER,
    GLA_KEY_WIDTH, GLA_KEY_WIDTH, GLA_VAL_WIDTH, GLA_GATE_RANK, GLA_VAL_WIDTH,
    DSA_HEADS * DSA_HEAD_DIM, DSA_HEAD_DIM, DSA_HEAD_DIM,
    IDX_HEADS * IDX_DIM, IDX_DIM, IDX_HEADS,
)
IN_WIDTH = sum(IN_SIZES)

LANE, SUBLANE = 128, 8
VMEM_LIMIT = 48 * 1024 * 1024
INT_MIN = -2 ** 31


def _round_up(n, m):
    return (n + m - 1) // m * m


def _mm_kernel(x_ref, g_ref, w_ref, *rest, normalize, has_res):
    if has_res:
        r_ref, o_ref, xs_ref = rest
    else:
        o_ref, xs_ref = rest

    @pl.when(pl.program_id(1) == 0)
    def _():
        xf = x_ref[...]
        if normalize:
            xf = xf * lax.rsqrt(jnp.mean(xf * xf, axis=-1, keepdims=True) + EPS) * g_ref[...]
        xs_ref[...] = xf.astype(jnp.bfloat16)

    acc = jnp.dot(xs_ref[...], w_ref[...], preferred_element_type=jnp.float32)
    if has_res:
        acc = acc + r_ref[...]
    o_ref[...] = acc


def _matmul(x, w, gain=None, residual=None, tm=512, tn=256):
    m, k = x.shape
    n = w.shape[1]
    n_pad = _round_up(n, tn)
    wb = w.astype(jnp.bfloat16)
    if n_pad != n:
        wb = jnp.pad(wb, ((0, 0), (0, n_pad - n)))
    normalize = gain is not None
    g = (gain if normalize else jnp.ones((k,), jnp.float32)).reshape(1, k).astype(jnp.float32)
    has_res = residual is not None
    assert m % tm == 0
    in_specs = [
        pl.BlockSpec((tm, k), lambda i, j: (i, 0)),
        pl.BlockSpec((1, k), lambda i, j: (0, 0)),
        pl.BlockSpec((k, tn), lambda i, j: (0, j)),
    ]
    args = [x, g, wb]
    if has_res:
        assert n_pad == n
        in_specs.append(pl.BlockSpec((tm, tn), lambda i, j: (i, j)))
        args.append(residual)
    out = pl.pallas_call(
        functools.partial(_mm_kernel, normalize=normalize, has_res=has_res),
        grid=(m // tm, n_pad // tn),
        in_specs=in_specs,
        out_specs=pl.BlockSpec((tm, tn), lambda i, j: (i, j)),
        out_shape=jax.ShapeDtypeStruct((m, n_pad), jnp.float32),
        scratch_shapes=[pltpu.VMEM((tm, k), jnp.bfloat16)],
        compiler_params=pltpu.CompilerParams(
            dimension_semantics=("arbitrary", "arbitrary"), vmem_limit_bytes=VMEM_LIMIT),
        name="norm_matmul",
    )(*args)
    return out[:, :n] if n_pad != n else out


def _dsa_kernel(qT_ref, k_ref, vT_ref, iqT_ref, ik_ref, iwT_ref, qg_ref, kg_ref, ikg_ref, o_ref,
                qn_s, kn_s, ikn_s, iq_s, vT_s, key_s, *, seq, topk):
    f32, bf16, i32 = jnp.float32, jnp.bfloat16, jnp.int32
    nq = DSA_QBLOCK
    for h in range(DSA_HEADS):
        rows = slice(h * DSA_HEAD_DIM, (h + 1) * DSA_HEAD_DIM)
        qh = qT_ref[rows, :]
        ms = jnp.mean(qh * qh, axis=0, keepdims=True)
        qn_s[rows, :] = (qh * lax.rsqrt(ms + EPS) * qg_ref[...]).astype(bf16)
    kk = k_ref[...]
    kn_s[...] = (kk * lax.rsqrt(jnp.mean(kk * kk, axis=-1, keepdims=True) + EPS) * kg_ref[...]).astype(bf16)
    ikk = ik_ref[...]
    ikn_s[...] = (ikk * lax.rsqrt(jnp.mean(ikk * ikk, axis=-1, keepdims=True) + EPS) * ikg_ref[...]).astype(bf16)
    iq_s[...] = iqT_ref[...].astype(bf16)
    vT_s[...] = vT_ref[...].astype(bf16)
    iw_scale = IDX_HEADS ** -0.5 * IDX_DIM ** -0.5
    n_jbits = max(1, int(seq).bit_length())

    def count(mask):
        return jnp.sum(mask.astype(i32), axis=0, keepdims=True)

    for i in range(seq // nq):
        nk = nq * (i + 1)
        qs = slice(i * nq, (i + 1) * nq)
        kidx = lax.broadcasted_iota(i32, (nk, nq), 0)
        lane = lax.broadcasted_iota(i32, (nk, nq), 1)
        valid = (kidx < i * nq + CHUNK) | (lane >= CHUNK)
        if nk > topk:
            sc = jnp.zeros((nk, nq), f32)
            for h in range(IDX_HEADS):
                a = jnp.dot(ikn_s[:nk, :], iq_s[h * IDX_DIM:(h + 1) * IDX_DIM, qs], preferred_element_type=f32)
                sc = sc + jnp.maximum(a, 0.0) * (iwT_ref[h:h + 1, qs] * iw_scale)
            sc = jnp.where(sc == 0.0, 0.0, sc)
            bits = pltpu.bitcast(sc, i32)
            key = jnp.where(bits < 0, bits ^ 0x7FFFFFFF, bits)
            key_s[:nk, :] = jnp.where(valid, key, INT_MIN)

            t0 = jnp.where(count(key_s[:nk, :] >= 0) >= topk, 0, INT_MIN).astype(i32)

            def tbody(b, t):
                cand = t + jnp.left_shift(jnp.int32(1), 30 - b)
                return jnp.where(count(key_s[:nk, :] >= cand) >= topk, cand, t)

            thr = lax.fori_loop(0, 31, tbody, t0)
            keyv = key_s[:nk, :]
            gt = keyv > thr
            need = topk - count(gt)

            def jbody(b, j):
                cand = j + jnp.left_shift(jnp.int32(1), n_jbits - 1 - b)
                f = count((key_s[:nk, :] == thr) & (kidx < cand))
                return jnp.where(f <= need, cand, j)

            jlim = lax.fori_loop(0, n_jbits, jbody, jnp.zeros((1, nq), i32))
            sel = gt | ((keyv == thr) & (kidx < jlim))
        else:
            sel = valid
        for h in range(DSA_HEADS):
            rows = slice(h * DSA_HEAD_DIM, (h + 1) * DSA_HEAD_DIM)
            s = jnp.dot(kn_s[:nk, :], qn_s[rows, qs], preferred_element_type=f32) * DSA_HEAD_DIM ** -0.5
            s = jnp.where(sel, s, -jnp.inf)
            m = jnp.max(s, axis=0, keepdims=True)
            p = jnp.exp(s - m)
            l = jnp.sum(p, axis=0, keepdims=True)
            oT = jnp.dot(vT_s[:, :nk], p.astype(bf16), preferred_element_type=f32)
            o_ref[rows, qs] = oT / l


def dsa_attention(qT, k, vT, iqT, ik, iwT, q_norm, k_norm, ik_norm):
    bsz, _, seq = qT.shape
    assert seq % DSA_QBLOCK == 0
    topk = min(DSA_TOPK_MAX, seq // 4)
    hd, idd = DSA_HEAD_DIM, IDX_DIM
    bspec = lambda *shape: pl.BlockSpec((None,) + shape, lambda b: (b,) + (0,) * len(shape))
    cspec = lambda *shape: pl.BlockSpec(shape, lambda b: (0,) * len(shape))
    bf16 = jnp.bfloat16
    return pl.pallas_call(
        functools.partial(_dsa_kernel, seq=seq, topk=topk),
        grid=(bsz,),
        in_specs=[bspec(DSA_HEADS * hd, seq), bspec(seq, hd), bspec(hd, seq), bspec(IDX_HEADS * idd, seq),
                  bspec(seq, idd), bspec(IDX_HEADS, seq), cspec(hd, 1), cspec(1, hd), cspec(1, idd)],
        out_specs=bspec(DSA_HEADS * hd, seq),
        out_shape=jax.ShapeDtypeStruct((bsz, DSA_HEADS * hd, seq), jnp.float32),
        scratch_shapes=[pltpu.VMEM((DSA_HEADS * hd, seq), bf16), pltpu.VMEM((seq, hd), bf16),
                        pltpu.VMEM((seq, idd), bf16), pltpu.VMEM((IDX_HEADS * idd, seq), bf16),
                        pltpu.VMEM((hd, seq), bf16), pltpu.VMEM((seq, DSA_QBLOCK), jnp.int32)],
        compiler_params=pltpu.CompilerParams(dimension_semantics=("arbitrary",), vmem_limit_bytes=VMEM_LIMIT),
        name="dsa_attention",
    )(qT, k, vT, iqT, ik, iwT, q_norm.reshape(hd, 1), k_norm.reshape(1, hd), ik_norm.reshape(1, idd))


def dsa_mixer(q, k, v, iq, ik, iw, q_norm, k_norm, ik_norm):
    t = lambda a: jnp.swapaxes(a, 1, 2)
    return t(dsa_attention(t(q), k, t(v), t(iq), ik, t(iw), q_norm, k_norm, ik_norm))


def _top16(vals, payload=None):
    n = vals.shape[0]
    iota = lax.broadcasted_iota(jnp.int32, vals.shape, 0)
    tv, ti = [], []
    for _ in range(PEER_TOPK):
        m = jnp.max(vals, axis=0, keepdims=True)
        idx = jnp.min(jnp.where(vals == m, iota, n), axis=0, keepdims=True)
        hit = iota == idx
        tv.append(m)
        ti.append(idx if payload is None else jnp.max(jnp.where(hit, payload, -1), axis=0, keepdims=True))
        vals = jnp.where(hit, -jnp.inf, vals)
    return jnp.concatenate(tv, axis=0), jnp.concatenate(ti, axis=0)


def _candidates(s1, i1, s2, i2):
    sub = lax.broadcasted_iota(jnp.int32, (SUBLANE, LANE), 0)
    cv, ci = [s1[0:1] + s2], [i1[0:1] * PEER_NKEYS + i2]
    for i in range(1, SUBLANE):
        v = s1[i:i + 1] + s2[0:SUBLANE]
        cv.append(jnp.where(sub < PEER_TOPK // (i + 1), v, -jnp.inf))
        ci.append(i1[i:i + 1] * PEER_NKEYS + i2[0:SUBLANE])
    cv.append(s1[SUBLANE:] + s2[0:1])
    ci.append(i1[SUBLANE:] * PEER_NKEYS + i2[0:1])
    return jnp.concatenate(cv, axis=0), jnp.concatenate(ci, axis=0)


def _peer_select_kernel(x_ref, g_ref, wqT_ref, sk_ref, hn_ref, off_ref, par_ref, gate_ref, qT_s, *, tb):
    f32, bf16 = jnp.float32, jnp.bfloat16
    x = x_ref[...]
    hn = x * lax.rsqrt(jnp.mean(x * x, axis=-1, keepdims=True) + EPS) * g_ref[...]
    hn_ref[...] = hn
    qT_s[...] = lax.dot_general(wqT_ref[...], hn.astype(bf16), (((1,), (1,)), ((), ())),
                                preferred_element_type=f32)

    def head(h, carry):
        for ct in range(tb // LANE):
            cols = slice(ct * LANE, (ct + 1) * LANE)
            tops = []
            for p in range(2):
                hp = h * 2 + p
                qhp = qT_s[pl.ds(pl.multiple_of(hp * PEER_HALF, PEER_HALF), PEER_HALF), cols]
                s = jnp.dot(sk_ref[hp], qhp.astype(bf16), preferred_element_type=f32)
                tops.append(_top16(s))
            (s1, i1), (s2, i2) = tops
            cand, cidx = _candidates(s1, i1, s2, i2)
            top, eidx = _top16(cand, cidx)
            g = jnp.exp(top - top[0:1])
            g = g / jnp.sum(g, axis=0, keepdims=True)
            rows = pl.ds(pl.multiple_of(h * PEER_TOPK, PEER_TOPK), PEER_TOPK)
            off_ref[rows, cols] = (eidx >> 1) * SUBLANE
            par_ref[rows, cols] = eidx & 1
            gate_ref[rows, cols] = g
        return carry

    lax.fori_loop(0, PEER_HEADS, head, 0)


def peer_select(x2, gain, w_query, sub_keys, tb=256):
    n, dm = x2.shape
    nq = w_query.shape[1]
    wqT = w_query.T.astype(jnp.bfloat16)
    sk = sub_keys.reshape(PEER_HEADS * 2, PEER_NKEYS, PEER_HALF).astype(jnp.bfloat16)
    return pl.pallas_call(
        functools.partial(_peer_select_kernel, tb=tb),
        grid=(n // tb,),
        in_specs=[pl.BlockSpec((tb, dm), lambda i: (i, 0)), pl.BlockSpec((1, dm), lambda i: (0, 0)),
                  pl.BlockSpec((nq, dm), lambda i: (0, 0)),
                  pl.BlockSpec((PEER_HEADS * 2, PEER_NKEYS, PEER_HALF), lambda i: (0, 0, 0))],
        out_specs=[pl.BlockSpec((tb, dm), lambda i: (i, 0))] + [pl.BlockSpec((PEER_SLOTS, tb), lambda i: (0, i))] * 3,
        out_shape=[jax.ShapeDtypeStruct((n, dm), jnp.float32), jax.ShapeDtypeStruct((PEER_SLOTS, n), jnp.int32),
                   jax.ShapeDtypeStruct((PEER_SLOTS, n), jnp.int32), jax.ShapeDtypeStruct((PEER_SLOTS, n), jnp.float32)],
        scratch_shapes=[pltpu.VMEM((nq, tb), jnp.float32)],
        compiler_params=pltpu.CompilerParams(dimension_semantics=("arbitrary",), vmem_limit_bytes=VMEM_LIMIT),
        name="peer_select",
    )(x2, gain.reshape(1, dm), wqT, sk)


def pack_table(tab):
    e, dm = tab.shape
    b = lax.bitcast_convert_type(tab.astype(jnp.bfloat16), jnp.uint16).astype(jnp.uint32)
    packed = (b[1::2] << 16) | b[0::2]
    return lax.bitcast_convert_type(packed, jnp.int32).reshape(e // 2 * (dm // LANE), LANE)


SPLIT = 3
PEER_BUFS = 4


def _gather_tiles(off_ref, tab_ref, g_ref, t):
    offs = off_ref.at[pl.ds(t * PEER_SLOTS, PEER_SLOTS)]
    for s in range(PEER_SLOTS):
        g_ref[s * SUBLANE:(s + 1) * SUBLANE, :] = tab_ref[pl.ds(pl.multiple_of(offs[s], SUBLANE), SUBLANE), :]


def _tree_sum(xs):
    while len(xs) > 1:
        xs = [xs[i] + xs[i + 1] for i in range(0, len(xs), 2)]
    return xs[0]


def _split_bf16(x):
    parts = []
    for _ in range(SPLIT):
        p = x.astype(jnp.bfloat16)
        parts.append(p)
        x = x - p.astype(jnp.float32)
    return parts


def _sub_weights(g_ref, sub):
    return pltpu.bitcast(g_ref[pl.ds(sub, PEER_SLOTS, stride=SUBLANE), :], jnp.bfloat16)


def _pipelined_tokens(off_ref, tab_ref, bufs, dots, tb):
    def token_quad(i, carry):
        t = 4 * i
        dots(jnp.maximum(t - 2, 0), bufs[2])
        dots(jnp.maximum(t - 1, 1), bufs[3])
        _gather_tiles(off_ref, tab_ref, bufs[0], t)
        _gather_tiles(off_ref, tab_ref, bufs[1], t + 1)
        dots(t, bufs[0])
        dots(t + 1, bufs[1])
        _gather_tiles(off_ref, tab_ref, bufs[2], t + 2)
        _gather_tiles(off_ref, tab_ref, bufs[3], t + 3)
        return carry

    lax.fori_loop(0, tb // PEER_BUFS, token_quad, 0)
    dots(tb - 2, bufs[2])
    dots(tb - 1, bufs[3])


def _zero_stale_buffers(bufs):
    @pl.when(pl.program_id(0) == 0)
    def _():
        for buf in bufs[2:]:
            buf[...] = jnp.zeros(buf.shape, buf.dtype)


def _peer_act_kernel(off_ref, h_ref, tab_ref, act_ref, *bufs, tb):
    f32 = jnp.float32
    row = lax.broadcasted_iota(jnp.int32, (SPLIT * SUBLANE, LANE), 0) % SUBLANE
    _zero_stale_buffers(bufs)

    def dots(t, buf):
        ht = h_ref[pl.ds(pl.multiple_of(t * SUBLANE, SUBLANE), SUBLANE), :]
        lhs = jnp.concatenate([p.astype(f32) for p in _split_bf16(ht)], axis=0)
        parts = [lax.dot_general(jnp.where(row == sub, lhs, 0.0).astype(jnp.bfloat16), _sub_weights(buf, sub),
                                 (((1,), (1,)), ((), ())), preferred_element_type=f32) for sub in range(SUBLANE)]
        act_ref[pl.ds(t, 1), :] = jnp.sum(_tree_sum(parts), axis=0, keepdims=True)

    _pipelined_tokens(off_ref, tab_ref, bufs, dots, tb)


def peer_act(off, hn, tab_u, tb=128):
    n, dm = hn.shape
    rows = dm // LANE
    assert tb % PEER_BUFS == 0 and n % tb == 0
    return pl.pallas_call(
        functools.partial(_peer_act_kernel, tb=tb),
        grid=(n // tb,),
        in_specs=[pl.BlockSpec((tb * PEER_SLOTS,), lambda i: (i,), memory_space=pltpu.SMEM),
                  pl.BlockSpec((tb * rows, LANE), lambda i: (i, 0)),
                  pl.BlockSpec(tab_u.shape, lambda i: (0, 0), pipeline_mode=pl.Buffered(1))],
        out_specs=pl.BlockSpec((tb, 2 * PEER_SLOTS), lambda i: (i, 0)),
        out_shape=jax.ShapeDtypeStruct((n, 2 * PEER_SLOTS), jnp.float32),
        scratch_shapes=[pltpu.VMEM((PEER_SLOTS * SUBLANE, LANE), jnp.int32)] * PEER_BUFS,
        compiler_params=pltpu.CompilerParams(dimension_semantics=("arbitrary",), vmem_limit_bytes=VMEM_LIMIT),
        name="peer_act",
    )(off, hn.reshape(n * rows, LANE), tab_u)


def _peer_out_kernel(off_ref, act_ref, gate_ref, x_ref, tab_ref, o_ref, c_s, *bufs, tb):
    f32 = jnp.float32
    row = lax.broadcasted_iota(jnp.int32, (SPLIT * SUBLANE, 2 * PEER_SLOTS), 0) % SUBLANE
    _zero_stale_buffers(bufs)
    c_s[...] = gate_ref[...] * jax.nn.gelu(act_ref[...])

    def dots(t, buf):
        c = c_s[pl.ds(t, 1), :]
        lhs = jnp.concatenate([jnp.broadcast_to(p.astype(f32), (SUBLANE, 2 * PEER_SLOTS)) for p in _split_bf16(c)], axis=0)
        acc = _tree_sum([jnp.dot(jnp.where(row == sub, lhs, 0.0).astype(jnp.bfloat16), _sub_weights(buf, sub),
                                 preferred_element_type=f32) for sub in range(SUBLANE)])
        rows = pl.ds(pl.multiple_of(t * SUBLANE, SUBLANE), SUBLANE)
        o_ref[rows, :] = x_ref[rows, :] + ((acc[0:SUBLANE] + acc[SUBLANE:2 * SUBLANE]) + acc[2 * SUBLANE:])

    _pipelined_tokens(off_ref, tab_ref, bufs, dots, tb)


def peer_out(off, act2, gate2, x2, tab_v, tb=128):
    n, dm = x2.shape
    rows = dm // LANE
    assert tb % PEER_BUFS == 0 and n % tb == 0
    out = pl.pallas_call(
        functools.partial(_peer_out_kernel, tb=tb),
        grid=(n // tb,),
        in_specs=[pl.BlockSpec((tb * PEER_SLOTS,), lambda i: (i,), memory_space=pltpu.SMEM),
                  pl.BlockSpec((tb, 2 * PEER_SLOTS), lambda i: (i, 0)),
                  pl.BlockSpec((tb, 2 * PEER_SLOTS), lambda i: (i, 0)),
                  pl.BlockSpec((tb * rows, LANE), lambda i: (i, 0)),
                  pl.BlockSpec(tab_v.shape, lambda i: (0, 0), pipeline_mode=pl.Buffered(1))],
        out_specs=pl.BlockSpec((tb * rows, LANE), lambda i: (i, 0)),
        out_shape=jax.ShapeDtypeStruct((n * rows, LANE), jnp.float32),
        scratch_shapes=[pltpu.VMEM((tb, 2 * PEER_SLOTS), jnp.float32)] + [
                        pltpu.VMEM((PEER_SLOTS * SUBLANE, LANE), jnp.int32)] * PEER_BUFS,
        compiler_params=pltpu.CompilerParams(dimension_semantics=("arbitrary",), vmem_limit_bytes=VMEM_LIMIT),
        name="peer_out",
    )(off, act2, gate2, x2.reshape(n * rows, LANE), tab_v)
    return out.reshape(n, dm)


def peer_layer(x2, gain, w_query, sub_keys, tab_u, tab_v):
    n = x2.shape[0]
    hn, offT, parT, gateT = peer_select(x2, gain, w_query, sub_keys)
    off = offT.T.reshape(n * PEER_SLOTS)
    par, gate = parT.T, gateT.T
    gate2 = jnp.stack([jnp.where(par == 0, gate, 0.0), jnp.where(par == 1, gate, 0.0)], axis=-1).reshape(n, 2 * PEER_SLOTS)
    act2 = peer_act(off, hn, tab_u)
    return peer_out(off, act2, gate2, x2, tab_v)


def rms_norm(x, g):
    xf = x.astype(jnp.float32)
    y = xf * lax.rsqrt(jnp.mean(xf * xf, axis=-1, keepdims=True) + EPS)
    return (y * g.astype(jnp.float32)).astype(x.dtype)


def l2_norm(x):
    xf = x.astype(jnp.float32)
    return xf * lax.rsqrt(jnp.sum(xf * xf, axis=-1, keepdims=True) + EPS)


def causal_dwconv(x, w):
    k = w.shape[0]
    return lax.conv_general_dilated(
        x, w[:, None, :].astype(x.dtype), window_strides=(1,), padding=[(k - 1, 0)],
        dimension_numbers=("NWC", "WIO", "NWC"), feature_group_count=x.shape[-1])


def to_chunks(t):
    bsz, seq, nh = t.shape[:3]
    t = t.reshape((bsz, seq // CHUNK, CHUNK, nh) + t.shape[3:])
    return jnp.moveaxis(jnp.moveaxis(t, 3, 2), 1, 0)


def from_chunks(t):
    nc, bsz, nh, q = t.shape[:4]
    t = jnp.moveaxis(jnp.moveaxis(t, 0, 1), 2, 3)
    return t.reshape((bsz, nc * q, nh) + t.shape[4:])


def ssd_scan(xs, dt, a, bm, cm):
    bsz, seq, nh, hp = xs.shape
    nc = seq // CHUNK
    xc = xs.reshape(bsz, nc, CHUNK, nh, hp)
    bc = bm.reshape(bsz, nc, CHUNK, nh, -1)
    cc = cm.reshape(bsz, nc, CHUNK, nh, -1)
    dtc = jnp.moveaxis(dt.reshape(bsz, nc, CHUNK, nh), 2, 3)
    acum = jnp.cumsum(dtc * a[:, None], axis=-1)
    causal = jnp.tril(jnp.ones((CHUNK, CHUNK), bool))
    seg = jnp.exp(jnp.where(causal, acum[..., :, None] - acum[..., None, :], -jnp.inf))
    cb = jnp.einsum("bcthn,bcshn->bchts", cc, bc)
    y_diag = jnp.einsum("bchts,bcshp->bcthp", cb * seg * dtc[..., None, :], xc)
    decay_end = jnp.exp(acum[..., -1:] - acum) * dtc
    states = jnp.einsum("bcshn,bchs,bcshp->bchpn", bc, decay_end, xc)
    chunk_decay = jnp.exp(acum[..., -1])

    def step(h, inp):
        st, dec = inp
        return h * dec[..., None, None] + st, h

    h0 = jnp.zeros((bsz, nh, hp, bc.shape[-1]), states.dtype)
    _, h_in = lax.scan(step, h0, (jnp.moveaxis(states, 1, 0), jnp.moveaxis(chunk_decay, 1, 0)))
    h_in = jnp.moveaxis(h_in, 0, 1)
    y_off = jnp.einsum("bcthn,bchpn,bcht->bcthp", cc, h_in, jnp.exp(acum))
    return (y_diag + y_off).reshape(bsz, seq, nh, hp)


def ssd_mixer(z, xbc, dt, conv_w, conv_b, dt_bias, a_log, d_skip, norm_g):
    bsz, seq, _ = z.shape
    f32 = jnp.float32
    xbc = jax.nn.silu(causal_dwconv(xbc, conv_w) + conv_b).astype(f32)
    gs = SSD_GROUPS * SSD_STATE
    rep = SSD_HEADS // SSD_GROUPS
    xs = xbc[..., :SSD_INNER].reshape(bsz, seq, SSD_HEADS, SSD_HEAD_DIM)
    bm = jnp.repeat(xbc[..., SSD_INNER:SSD_INNER + gs].reshape(bsz, seq, SSD_GROUPS, SSD_STATE), rep, axis=2)
    cm = jnp.repeat(xbc[..., SSD_INNER + gs:].reshape(bsz, seq, SSD_GROUPS, SSD_STATE), rep, axis=2)
    dt = jax.nn.softplus(dt.astype(f32) + dt_bias.astype(f32))
    a = -jnp.exp(a_log.astype(f32))
    y = ssd_scan(xs, dt, a, bm, cm) + xs * d_skip.astype(f32)[:, None]
    y = y.reshape(bsz, seq, SSD_INNER) * jax.nn.silu(z.astype(f32))
    return rms_norm(y, norm_g).astype(z.dtype)


def gated_delta_scan(q, k, v, beta, g):
    bsz, seq, nh, dk = q.shape
    dv = v.shape[-1]
    qc, kc, vc = to_chunks(q), to_chunks(k), to_chunks(v)
    bc, gcum = to_chunks(beta), jnp.cumsum(to_chunks(g), axis=-1)
    incl = jnp.tril(jnp.ones((CHUNK, CHUNK), bool))
    strict = jnp.tril(jnp.ones((CHUNK, CHUNK), bool), k=-1)
    decay = jnp.exp(jnp.where(incl, gcum[..., :, None] - gcum[..., None, :], -jnp.inf))
    kb = kc * bc[..., None]
    a_mat = jnp.where(strict, jnp.einsum("nbhid,nbhjd->nbhij", kb, kc) * decay, 0.0)
    eye = jnp.eye(CHUNK, dtype=jnp.float32)
    t_mat = lax.linalg.triangular_solve(eye + a_mat, jnp.broadcast_to(eye, a_mat.shape),
                                        left_side=True, lower=True)
    u = jnp.einsum("nbhij,nbhjd->nbhid", t_mat, vc * bc[..., None])
    w = jnp.einsum("nbhij,nbhjd->nbhid", t_mat, kb * jnp.exp(gcum)[..., None])
    qk = jnp.where(incl, jnp.einsum("nbhid,nbhjd->nbhij", qc, kc) * decay, 0.0)
    q_dec = qc * jnp.exp(gcum)[..., None]
    k_dec = kc * jnp.exp(gcum[..., -1:] - gcum)[..., None]
    last = jnp.exp(gcum[..., -1])

    def step(s, inp):
        u_c, w_c, qk_c, qd_c, kd_c, l_c = inp
        v_new = u_c - jnp.einsum("bhid,bhde->bhie", w_c, s)
        o = jnp.einsum("bhid,bhde->bhie", qd_c, s) + jnp.einsum("bhij,bhje->bhie", qk_c, v_new)
        s = s * l_c[..., None, None] + jnp.einsum("bhjd,bhje->bhde", kd_c, v_new)
        return s, o

    s0 = jnp.zeros((bsz, nh, dk, dv), jnp.float32)
    _, o = lax.scan(step, s0, (u, w, qk, q_dec, k_dec, last))
    return from_chunks(o)


def gdn_mixer(qkv, a, b, gate, conv_w, a_log, dt_bias, norm_g):
    bsz, seq, _ = qkv.shape
    f32 = jnp.float32
    out_dtype = qkv.dtype
    shp = (bsz, seq, GDN_HEADS, GDN_HEAD_DIM)
    qkv = jax.nn.silu(causal_dwconv(qkv, conv_w)).astype(f32)
    q = l2_norm(qkv[..., :GDN_INNER].reshape(shp)) * GDN_HEAD_DIM ** -0.5
    k = l2_norm(qkv[..., GDN_INNER:2 * GDN_INNER].reshape(shp))
    v = qkv[..., 2 * GDN_INNER:].reshape(shp)
    beta = jax.nn.sigmoid(b.astype(f32))
    g = -jnp.exp(a_log.astype(f32)) * jax.nn.softplus(a.astype(f32) + dt_bias.astype(f32))
    o = gated_delta_scan(q, k, v, beta, g)
    o = rms_norm(o, norm_g) * jax.nn.silu(gate.astype(f32).reshape(shp))
    return o.reshape(bsz, seq, GDN_INNER).astype(out_dtype)


def gla_scan(q, k, v, gk):
    bsz, seq, nh, dk = q.shape
    dv = v.shape[-1]
    gcum = jnp.cumsum(to_chunks(gk), axis=-2)
    incl = jnp.tril(jnp.ones((CHUNK, CHUNK), bool))[:, :, None]

    def step(s, inp):
        qc, kc, vc, gc = inp
        diff = gc[:, :, :, None, :] - gc[:, :, None, :, :]
        dec = jnp.exp(jnp.where(incl, diff, -jnp.inf))
        att = jnp.einsum("bhtd,bhsd,bhtsd->bhts", qc, kc, dec)
        o = jnp.einsum("bhts,bhse->bhte", att, vc) + jnp.einsum("bhtd,bhde->bhte", qc * jnp.exp(gc), s)
        s = s * jnp.exp(gc[:, :, -1, :])[..., None] + jnp.einsum(
            "bhsd,bhse->bhde", kc * jnp.exp(gc[:, :, -1:, :] - gc), vc)
        return s, o

    s0 = jnp.zeros((bsz, nh, dk, dv), jnp.float32)
    _, o = lax.scan(step, s0, (to_chunks(q), to_chunks(k), to_chunks(v), gcum))
    return from_chunks(o)


def gla_mixer(q, k, v, glr, r, w_gate2, b_gate, norm_g):
    bsz, seq, _ = q.shape
    f32 = jnp.float32
    kshape = (bsz, seq, GLA_HEADS, GLA_KEY_DIM)
    vshape = (bsz, seq, GLA_HEADS, GLA_VAL_DIM)
    gk = jax.nn.log_sigmoid((glr @ w_gate2 + b_gate).astype(f32)) / GLA_GATE_NORMALIZER
    o = gla_scan(q.astype(f32).reshape(kshape) * GLA_KEY_DIM ** -0.5, k.astype(f32).reshape(kshape),
                 v.astype(f32).reshape(vshape), gk.reshape(kshape))
    o = rms_norm(o, norm_g) * jax.nn.silu(r.astype(f32).reshape(vshape))
    return o.reshape(bsz, seq, GLA_VAL_WIDTH).astype(q.dtype)


def memory_xattn_core(q, m, wk, wv, q_norm, k_norm):
    bsz, seq, _ = q.shape
    nm = m.shape[1]
    q = rms_norm(q.reshape(bsz, seq, MEM_HEADS, MEM_HEAD_DIM), q_norm)
    k = rms_norm((m @ wk).reshape(bsz, nm, MEM_HEADS, MEM_HEAD_DIM), k_norm)
    v = (m @ wv).reshape(bsz, nm, MEM_HEADS, MEM_HEAD_DIM)
    s = jnp.einsum("bthd,bshd->bhts", q, k).astype(jnp.float32) * MEM_HEAD_DIM ** -0.5
    p = jax.nn.softmax(s, axis=-1).astype(v.dtype)
    return jnp.einsum("bhts,bshd->bthd", p, v).reshape(bsz, seq, MEM_WIDTH)


def kernel(x, mem, mix_norm, w_in, ssd_conv_w, ssd_conv_b, ssd_dt_bias, ssd_a_log, ssd_d, ssd_norm,
           gdn_conv_w, gdn_a_log, gdn_dt_bias, gdn_norm, gla_w_gate, gla_b_gate, gla_norm,
           dsa_q_norm, dsa_k_norm, idx_k_norm, w_out, xattn_norm, mem_norm, xattn_wq, xattn_wk,
           xattn_wv, xattn_wo, xattn_q_norm, xattn_k_norm, ffn_norm, peer_w_query, peer_sub_keys,
           peer_u, peer_v):
    bsz, seq, dm = x.shape
    depth = w_in.shape[0]
    splits = np.cumsum(np.array(IN_SIZES))[:-1].tolist()
    x2 = x.reshape(bsz * seq, dm)
    for l in range(depth):
        p = _matmul(x2, w_in[l], gain=mix_norm[l]).reshape(bsz, seq, IN_WIDTH)
        (ssd_z, ssd_xbc, ssd_dt, gdn_qkv, gdn_a, gdn_b, gdn_g, gla_q, gla_k, gla_v, gla_glr, gla_r,
         dsa_q, dsa_k, dsa_v, idx_q, idx_k, idx_w) = jnp.split(p, splits, axis=-1)
        y_a = ssd_mixer(ssd_z, ssd_xbc, ssd_dt, ssd_conv_w[l], ssd_conv_b[l], ssd_dt_bias[l],
                        ssd_a_log[l], ssd_d[l], ssd_norm[l])
        y_b = gdn_mixer(gdn_qkv, gdn_a, gdn_b, gdn_g, gdn_conv_w[l], gdn_a_log[l], gdn_dt_bias[l], gdn_norm[l])
        y_c = gla_mixer(gla_q, gla_k, gla_v, gla_glr, gla_r, gla_w_gate[l], gla_b_gate[l], gla_norm[l])
        y_d = dsa_mixer(dsa_q, dsa_k, dsa_v, idx_q, idx_k, idx_w, dsa_q_norm[l], dsa_k_norm[l], idx_k_norm[l])
        y = jnp.concatenate([y_a, y_b, y_c, y_d], axis=-1).reshape(bsz * seq, -1)
        x2 = _matmul(y, w_out[l], residual=x2)
        q = _matmul(x2, xattn_wq[l], gain=xattn_norm[l]).reshape(bsz, seq, MEM_WIDTH)
        o = memory_xattn_core(q, rms_norm(mem, mem_norm[l]), xattn_wk[l], xattn_wv[l],
                              xattn_q_norm[l], xattn_k_norm[l])
        x2 = _matmul(o.reshape(bsz * seq, MEM_WIDTH), xattn_wo[l], residual=x2)
        x2 = peer_layer(x2, ffn_norm[l], peer_w_query[l], peer_sub_keys[l],
                        pack_table(peer_u[l]), pack_table(peer_v[l]))
    return x2.reshape(bsz, seq, dm)
```

```python
import functools

import jax
import jax.numpy as jnp
import numpy as np
from jax import lax
from jax.experimental import pallas as pl
from jax.experimental.pallas import tpu as pltpu

CHUNK = 64
EPS = 1e-6
SSD_HEADS, SSD_HEAD_DIM, SSD_GROUPS, SSD_STATE, SSD_CONV = 4, 64, 2, 64, 4
SSD_INNER = SSD_HEADS * SSD_HEAD_DIM
SSD_CONV_DIM = SSD_INNER + 2 * SSD_GROUPS * SSD_STATE
GDN_HEADS, GDN_HEAD_DIM, GDN_CONV = 4, 64, 4
GDN_INNER = GDN_HEADS * GDN_HEAD_DIM
GLA_HEADS, GLA_KEY_DIM, GLA_VAL_DIM, GLA_GATE_RANK = 4, 32, 64, 16
GLA_KEY_WIDTH = GLA_HEADS * GLA_KEY_DIM
GLA_VAL_WIDTH = GLA_HEADS * GLA_VAL_DIM
GLA_GATE_NORMALIZER = 16.0
DSA_HEADS, DSA_HEAD_DIM, IDX_HEADS, IDX_DIM = 4, 64, 8, 32
DSA_TOPK_MAX, DSA_QBLOCK = 256, 128
MEM_HEADS, MEM_HEAD_DIM = 4, 64
MEM_WIDTH = MEM_HEADS * MEM_HEAD_DIM
PEER_HEADS, PEER_NKEYS, PEER_QDIM, PEER_TOPK = 8, 128, 128, 16
PEER_HALF = PEER_QDIM // 2
PEER_SLOTS = PEER_HEADS * PEER_TOPK

IN_SIZES = (
    SSD_INNER, SSD_CONV_DIM, SSD_HEADS,
    3 * GDN_INNER, GDN_HEADS, GDN_HEADS, GDN_INNER,
    GLA_KEY_WIDTH, GLA_KEY_WIDTH, GLA_VAL_WIDTH, GLA_GATE_RANK, GLA_VAL_WIDTH,
    DSA_HEADS * DSA_HEAD_DIM, DSA_HEAD_DIM, DSA_HEAD_DIM,
    IDX_HEADS * IDX_DIM, IDX_DIM, IDX_HEADS,
)
IN_WIDTH = sum(IN_SIZES)

LANE, SUBLANE = 128, 8
VMEM_LIMIT = 48 * 1024 * 1024
INT_MIN = -2 ** 31


def _round_up(n, m):
    return (n + m - 1) // m * m


def _mm_kernel(x_ref, g_ref, w_ref, *rest, normalize, has_res):
    if has_res:
        r_ref, o_ref, xs_ref = rest
    else:
        o_ref, xs_ref = rest

    @pl.when(pl.program_id(1) == 0)
    def _():
        xf = x_ref[...]
        if normalize:
            xf = xf * lax.rsqrt(jnp.mean(xf * xf, axis=-1, keepdims=True) + EPS) * g_ref[...]
        xs_ref[...] = xf.astype(jnp.bfloat16)

    acc = jnp.dot(xs_ref[...], w_ref[...], preferred_element_type=jnp.float32)
    if has_res:
        acc = acc + r_ref[...]
    o_ref[...] = acc


def _matmul(x, w, gain=None, residual=None, tm=512, tn=256):
    m, k = x.shape
    n = w.shape[1]
    n_pad = _round_up(n, tn)
    wb = w.astype(jnp.bfloat16)
    if n_pad != n:
        wb = jnp.pad(wb, ((0, 0), (0, n_pad - n)))
    normalize = gain is not None
    g = (gain if normalize else jnp.ones((k,), jnp.float32)).reshape(1, k).astype(jnp.float32)
    has_res = residual is not None
    assert m % tm == 0
    in_specs = [
        pl.BlockSpec((tm, k), lambda i, j: (i, 0)),
        pl.BlockSpec((1, k), lambda i, j: (0, 0)),
        pl.BlockSpec((k, tn), lambda i, j: (0, j)),
    ]
    args = [x, g, wb]
    if has_res:
        assert n_pad == n
        in_specs.append(pl.BlockSpec((tm, tn), lambda i, j: (i, j)))
        args.append(residual)
    out = pl.pallas_call(
        functools.partial(_mm_kernel, normalize=normalize, has_res=has_res),
        grid=(m // tm, n_pad // tn),
        in_specs=in_specs,
        out_specs=pl.BlockSpec((tm, tn), lambda i, j: (i, j)),
        out_shape=jax.ShapeDtypeStruct((m, n_pad), jnp.float32),
        scratch_shapes=[pltpu.VMEM((tm, k), jnp.bfloat16)],
        compiler_params=pltpu.CompilerParams(
            dimension_semantics=("arbitrary", "arbitrary"), vmem_limit_bytes=VMEM_LIMIT),
        name="norm_matmul",
    )(*args)
    return out[:, :n] if n_pad != n else out


def _dsa_kernel(qT_ref, k_ref, vT_ref, iqT_ref, ik_ref, iwT_ref, qg_ref, kg_ref, ikg_ref, o_ref,
                qn_s, kn_s, ikn_s, iq_s, vT_s, key_s, *, seq, topk):
    f32, bf16, i32 = jnp.float32, jnp.bfloat16, jnp.int32
    nq = DSA_QBLOCK
    for h in range(DSA_HEADS):
        rows = slice(h * DSA_HEAD_DIM, (h + 1) * DSA_HEAD_DIM)
        qh = qT_ref[rows, :]
        ms = jnp.mean(qh * qh, axis=0, keepdims=True)
        qn_s[rows, :] = (qh * lax.rsqrt(ms + EPS) * qg_ref[...]).astype(bf16)
    kk = k_ref[...]
    kn_s[...] = (kk * lax.rsqrt(jnp.mean(kk * kk, axis=-1, keepdims=True) + EPS) * kg_ref[...]).astype(bf16)
    ikk = ik_ref[...]
    ikn_s[...] = (ikk * lax.rsqrt(jnp.mean(ikk * ikk, axis=-1, keepdims=True) + EPS) * ikg_ref[...]).astype(bf16)
    iq_s[...] = iqT_ref[...].astype(bf16)
    vT_s[...] = vT_ref[...].astype(bf16)
    iw_scale = IDX_HEADS ** -0.5 * IDX_DIM ** -0.5
    n_jbits = max(1, int(seq).bit_length())

    def count(mask):
        return jnp.sum(mask.astype(i32), axis=0, keepdims=True)

    for i in range(seq // nq):
        nk = nq * (i + 1)
        qs = slice(i * nq, (i + 1) * nq)
        kidx = lax.broadcasted_iota(i32, (nk, nq), 0)
        lane = lax.broadcasted_iota(i32, (nk, nq), 1)
        valid = (kidx < i * nq + CHUNK) | (lane >= CHUNK)
        if nk > topk:
            sc = jnp.zeros((nk, nq), f32)
            for h in range(IDX_HEADS):
                a = jnp.dot(ikn_s[:nk, :], iq_s[h * IDX_DIM:(h + 1) * IDX_DIM, qs], preferred_element_type=f32)
                sc = sc + jnp.maximum(a, 0.0) * (iwT_ref[h:h + 1, qs] * iw_scale)
            sc = jnp.where(sc == 0.0, 0.0, sc)
            bits = pltpu.bitcast(sc, i32)
            key = jnp.where(bits < 0, bits ^ 0x7FFFFFFF, bits)
            key_s[:nk, :] = jnp.where(valid, key, INT_MIN)

            t0 = jnp.where(count(key_s[:nk, :] >= 0) >= topk, 0, INT_MIN).astype(i32)

            def tbody(b, t):
                cand = t + jnp.left_shift(jnp.int32(1), 30 - b)
                return jnp.where(count(key_s[:nk, :] >= cand) >= topk, cand, t)

            thr = lax.fori_loop(0, 31, tbody, t0)
            keyv = key_s[:nk, :]
            gt = keyv > thr
            need = topk - count(gt)

            def jbody(b, j):
                cand = j + jnp.left_shift(jnp.int32(1), n_jbits - 1 - b)
                f = count((key_s[:nk, :] == thr) & (kidx < cand))
                return jnp.where(f <= need, cand, j)

            jlim = lax.fori_loop(0, n_jbits, jbody, jnp.zeros((1, nq), i32))
            sel = gt | ((keyv == thr) & (kidx < jlim))
        else:
            sel = valid
        for h in range(DSA_HEADS):
            rows = slice(h * DSA_HEAD_DIM, (h + 1) * DSA_HEAD_DIM)
            s = jnp.dot(kn_s[:nk, :], qn_s[rows, qs], preferred_element_type=f32) * DSA_HEAD_DIM ** -0.5
            s = jnp.where(sel, s, -jnp.inf)
            m = jnp.max(s, axis=0, keepdims=True)
            p = jnp.exp(s - m)
            l = jnp.sum(p, axis=0, keepdims=True)
            oT = jnp.dot(vT_s[:, :nk], p.astype(bf16), preferred_element_type=f32)
            o_ref[rows, qs] = oT / l


def dsa_attention(qT, k, vT, iqT, ik, iwT, q_norm, k_norm, ik_norm):
    bsz, _, seq = qT.shape
    assert seq % DSA_QBLOCK == 0
    topk = min(DSA_TOPK_MAX, seq // 4)
    hd, idd = DSA_HEAD_DIM, IDX_DIM
    bspec = lambda *shape: pl.BlockSpec((None,) + shape, lambda b: (b,) + (0,) * len(shape))
    cspec = lambda *shape: pl.BlockSpec(shape, lambda b: (0,) * len(shape))
    bf16 = jnp.bfloat16
    return pl.pallas_call(
        functools.partial(_dsa_kernel, seq=seq, topk=topk),
        grid=(bsz,),
        in_specs=[bspec(DSA_HEADS * hd, seq), bspec(seq, hd), bspec(hd, seq), bspec(IDX_HEADS * idd, seq),
                  bspec(seq, idd), bspec(IDX_HEADS, seq), cspec(hd, 1), cspec(1, hd), cspec(1, idd)],
        out_specs=bspec(DSA_HEADS * hd, seq),
        out_shape=jax.ShapeDtypeStruct((bsz, DSA_HEADS * hd, seq), jnp.float32),
        scratch_shapes=[pltpu.VMEM((DSA_HEADS * hd, seq), bf16), pltpu.VMEM((seq, hd), bf16),
                        pltpu.VMEM((seq, idd), bf16), pltpu.VMEM((IDX_HEADS * idd, seq), bf16),
                        pltpu.VMEM((hd, seq), bf16), pltpu.VMEM((seq, DSA_QBLOCK), jnp.int32)],
        compiler_params=pltpu.CompilerParams(dimension_semantics=("arbitrary",), vmem_limit_bytes=VMEM_LIMIT),
        name="dsa_attention",
    )(qT, k, vT, iqT, ik, iwT, q_norm.reshape(hd, 1), k_norm.reshape(1, hd), ik_norm.reshape(1, idd))


def dsa_mixer(q, k, v, iq, ik, iw, q_norm, k_norm, ik_norm):
    t = lambda a: jnp.swapaxes(a, 1, 2)
    return t(dsa_attention(t(q), k, t(v), t(iq), ik, t(iw), q_norm, k_norm, ik_norm))


def _top16(vals, payload=None):
    n = vals.shape[0]
    iota = lax.broadcasted_iota(jnp.int32, vals.shape, 0)
    tv, ti = [], []
    for _ in range(PEER_TOPK):
        m = jnp.max(vals, axis=0, keepdims=True)
        idx = jnp.min(jnp.where(vals == m, iota, n), axis=0, keepdims=True)
        hit = iota == idx
        tv.append(m)
        ti.append(idx if payload is None else jnp.max(jnp.where(hit, payload, -1), axis=0, keepdims=True))
        vals = jnp.where(hit, -jnp.inf, vals)
    return jnp.concatenate(tv, axis=0), jnp.concatenate(ti, axis=0)


def _candidates(s1, i1, s2, i2):
    sub = lax.broadcasted_iota(jnp.int32, (SUBLANE, LANE), 0)
    cv, ci = [s1[0:1] + s2], [i1[0:1] * PEER_NKEYS + i2]
    for i in range(1, SUBLANE):
        v = s1[i:i + 1] + s2[0:SUBLANE]
        cv.append(jnp.where(sub < PEER_TOPK // (i + 1), v, -jnp.inf))
        ci.append(i1[i:i + 1] * PEER_NKEYS + i2[0:SUBLANE])
    cv.append(s1[SUBLANE:] + s2[0:1])
    ci.append(i1[SUBLANE:] * PEER_NKEYS + i2[0:1])
    return jnp.concatenate(cv, axis=0), jnp.concatenate(ci, axis=0)


def _peer_select_kernel(x_ref, g_ref, wqT_ref, sk_ref, hn_ref, off_ref, par_ref, gate_ref, qT_s, *, tb):
    f32, bf16 = jnp.float32, jnp.bfloat16
    x = x_ref[...]
    hn = x * lax.rsqrt(jnp.mean(x * x, axis=-1, keepdims=True) + EPS) * g_ref[...]
    hn_ref[...] = hn
    qT_s[...] = lax.dot_general(wqT_ref[...], hn.astype(bf16), (((1,), (1,)), ((), ())),
                                preferred_element_type=f32)

    def head(h, carry):
        for ct in range(tb // LANE):
            cols = slice(ct * LANE, (ct + 1) * LANE)
            tops = []
            for p in range(2):
                hp = h * 2 + p
                qhp = qT_s[pl.ds(pl.multiple_of(hp * PEER_HALF, PEER_HALF), PEER_HALF), cols]
                s = jnp.dot(sk_ref[hp], qhp.astype(bf16), preferred_element_type=f32)
                tops.append(_top16(s))
            (s1, i1), (s2, i2) = tops
            cand, cidx = _candidates(s1, i1, s2, i2)
            top, eidx = _top16(cand, cidx)
            g = jnp.exp(top - top[0:1])
            g = g / jnp.sum(g, axis=0, keepdims=True)
            rows = pl.ds(pl.multiple_of(h * PEER_TOPK, PEER_TOPK), PEER_TOPK)
            off_ref[rows, cols] = (eidx >> 1) * SUBLANE
            par_ref[rows, cols] = eidx & 1
            gate_ref[rows, cols] = g
        return carry

    lax.fori_loop(0, PEER_HEADS, head, 0)


def peer_select(x2, gain, w_query, sub_keys, tb=256):
    n, dm = x2.shape
    nq = w_query.shape[1]
    wqT = w_query.T.astype(jnp.bfloat16)
    sk = sub_keys.reshape(PEER_HEADS * 2, PEER_NKEYS, PEER_HALF).astype(jnp.bfloat16)
    return pl.pallas_call(
        functools.partial(_peer_select_kernel, tb=tb),
        grid=(n // tb,),
        in_specs=[pl.BlockSpec((tb, dm), lambda i: (i, 0)), pl.BlockSpec((1, dm), lambda i: (0, 0)),
                  pl.BlockSpec((nq, dm), lambda i: (0, 0)),
                  pl.BlockSpec((PEER_HEADS * 2, PEER_NKEYS, PEER_HALF), lambda i: (0, 0, 0))],
        out_specs=[pl.BlockSpec((tb, dm), lambda i: (i, 0))] + [pl.BlockSpec((PEER_SLOTS, tb), lambda i: (0, i))] * 3,
        out_shape=[jax.ShapeDtypeStruct((n, dm), jnp.float32), jax.ShapeDtypeStruct((PEER_SLOTS, n), jnp.int32),
                   jax.ShapeDtypeStruct((PEER_SLOTS, n), jnp.int32), jax.ShapeDtypeStruct((PEER_SLOTS, n), jnp.float32)],
        scratch_shapes=[pltpu.VMEM((nq, tb), jnp.float32)],
        compiler_params=pltpu.CompilerParams(dimension_semantics=("arbitrary",), vmem_limit_bytes=VMEM_LIMIT),
        name="peer_select",
    )(x2, gain.reshape(1, dm), wqT, sk)


def pack_table(tab):
    e, dm = tab.shape
    b = lax.bitcast_convert_type(tab.astype(jnp.bfloat16), jnp.uint16).astype(jnp.uint32)
    packed = (b[1::2] << 16) | b[0::2]
    return lax.bitcast_convert_type(packed, jnp.int32).reshape(e // 2 * (dm // LANE), LANE)


SPLIT = 3
PEER_BUFS = 4


def _gather_tiles(off_ref, tab_ref, g_ref, t):
    offs = off_ref.at[pl.ds(t * PEER_SLOTS, PEER_SLOTS)]
    for s in range(PEER_SLOTS):
        g_ref[s * SUBLANE:(s + 1) * SUBLANE, :] = tab_ref[pl.ds(pl.multiple_of(offs[s], SUBLANE), SUBLANE), :]


PEER_ROWS = 2 * PEER_SLOTS * SUBLANE


def _expand_matrix():
    k = np.arange(PEER_ROWS)
    c = np.arange(2 * PEER_SLOTS)
    return jnp.asarray(c[:, None] == (2 * (k // (2 * SUBLANE)) + k % 2)[None, :], jnp.bfloat16)


def _own_sublane(nrows):
    row = lax.broadcasted_iota(jnp.int32, (nrows, PEER_ROWS), 0) % SUBLANE
    col = lax.broadcasted_iota(jnp.int32, (nrows, PEER_ROWS), 1)
    return (col // 2) % SUBLANE == row


def _split_bf16(x):
    parts = []
    for _ in range(SPLIT):
        p = x.astype(jnp.bfloat16)
        parts.append(p)
        x = x - p.astype(jnp.float32)
    return parts


def _tile_rows(g_ref):
    return pltpu.bitcast(g_ref[...], jnp.bfloat16)


def _pipelined_tokens(off_ref, tab_ref, bufs, dots, tb):
    def token_quad(i, carry):
        t = 4 * i
        dots(jnp.maximum(t - 2, 0), bufs[2])
        dots(jnp.maximum(t - 1, 1), bufs[3])
        _gather_tiles(off_ref, tab_ref, bufs[0], t)
        _gather_tiles(off_ref, tab_ref, bufs[1], t + 1)
        dots(t, bufs[0])
        dots(t + 1, bufs[1])
        _gather_tiles(off_ref, tab_ref, bufs[2], t + 2)
        _gather_tiles(off_ref, tab_ref, bufs[3], t + 3)
        return carry

    lax.fori_loop(0, tb // PEER_BUFS, token_quad, 0)
    dots(tb - 2, bufs[2])
    dots(tb - 1, bufs[3])


def _zero_stale_buffers(bufs):
    @pl.when(pl.program_id(0) == 0)
    def _():
        for buf in bufs[2:]:
            buf[...] = jnp.zeros(buf.shape, buf.dtype)


def _peer_act_kernel(off_ref, h_ref, e_ref, tab_ref, act_ref, a_s, *bufs, tb):
    f32 = jnp.float32
    nt = (((1,), (1,)), ((), ()))
    mine = _own_sublane(SPLIT * SUBLANE)
    _zero_stale_buffers(bufs)

    def dots(t, buf):
        ht = h_ref[pl.ds(pl.multiple_of(t * SUBLANE, SUBLANE), SUBLANE), :]
        lhs = jnp.concatenate(_split_bf16(ht), axis=0)
        res = lax.dot_general(lhs, _tile_rows(buf), nt, preferred_element_type=f32)
        a_s[pl.ds(t, 1), :] = jnp.sum(jnp.where(mine, res, 0.0), axis=0, keepdims=True)

    _pipelined_tokens(off_ref, tab_ref, bufs, dots, tb)
    parts = lax.dot_general(jnp.concatenate(_split_bf16(a_s[...]), axis=0), e_ref[...], nt, preferred_element_type=f32)
    act_ref[...] = (parts[0:tb] + parts[tb:2 * tb]) + parts[2 * tb:]


def peer_act(off, hn, tab_u, tb=128):
    n, dm = hn.shape
    rows = dm // LANE
    assert tb % PEER_BUFS == 0 and n % tb == 0
    return pl.pallas_call(
        functools.partial(_peer_act_kernel, tb=tb),
        grid=(n // tb,),
        in_specs=[pl.BlockSpec((tb * PEER_SLOTS,), lambda i: (i,), memory_space=pltpu.SMEM),
                  pl.BlockSpec((tb * rows, LANE), lambda i: (i, 0)),
                  pl.BlockSpec((2 * PEER_SLOTS, PEER_ROWS), lambda i: (0, 0)),
                  pl.BlockSpec(tab_u.shape, lambda i: (0, 0), pipeline_mode=pl.Buffered(1))],
        out_specs=pl.BlockSpec((tb, 2 * PEER_SLOTS), lambda i: (i, 0)),
        out_shape=jax.ShapeDtypeStruct((n, 2 * PEER_SLOTS), jnp.float32),
        scratch_shapes=[pltpu.VMEM((tb, PEER_ROWS), jnp.float32)] + [
                        pltpu.VMEM((PEER_SLOTS * SUBLANE, LANE), jnp.int32)] * PEER_BUFS,
        compiler_params=pltpu.CompilerParams(dimension_semantics=("arbitrary",), vmem_limit_bytes=VMEM_LIMIT),
        name="peer_act",
    )(off, hn.reshape(n * rows, LANE), _expand_matrix(), tab_u)


def _peer_out_kernel(off_ref, act_ref, gate_ref, x_ref, e_ref, tab_ref, o_ref, cexp_s, *bufs, tb):
    f32 = jnp.float32
    mine = _own_sublane(SUBLANE)
    _zero_stale_buffers(bufs)
    c = gate_ref[...] * jax.nn.gelu(act_ref[...])
    cexp_s[...] = jnp.dot(jnp.concatenate(_split_bf16(c), axis=0), e_ref[...], preferred_element_type=f32)

    def dots(t, buf):
        lhs = jnp.concatenate([jnp.where(mine, cexp_s[pl.ds(part * tb + t, 1), :], 0.0) for part in range(SPLIT)], axis=0)
        acc = jnp.dot(lhs.astype(jnp.bfloat16), _tile_rows(buf), preferred_element_type=f32)
        rows = pl.ds(pl.multiple_of(t * SUBLANE, SUBLANE), SUBLANE)
        o_ref[rows, :] = x_ref[rows, :] + ((acc[0:SUBLANE] + acc[SUBLANE:2 * SUBLANE]) + acc[2 * SUBLANE:])

    _pipelined_tokens(off_ref, tab_ref, bufs, dots, tb)


def peer_out(off, act2, gate2, x2, tab_v, tb=128):
    n, dm = x2.shape
    rows = dm // LANE
    assert tb % PEER_BUFS == 0 and n % tb == 0
    out = pl.pallas_call(
        functools.partial(_peer_out_kernel, tb=tb),
        grid=(n // tb,),
        in_specs=[pl.BlockSpec((tb * PEER_SLOTS,), lambda i: (i,), memory_space=pltpu.SMEM),
                  pl.BlockSpec((tb, 2 * PEER_SLOTS), lambda i: (i, 0)),
                  pl.BlockSpec((tb, 2 * PEER_SLOTS), lambda i: (i, 0)),
                  pl.BlockSpec((tb * rows, LANE), lambda i: (i, 0)),
                  pl.BlockSpec((2 * PEER_SLOTS, PEER_ROWS), lambda i: (0, 0)),
                  pl.BlockSpec(tab_v.shape, lambda i: (0, 0), pipeline_mode=pl.Buffered(1))],
        out_specs=pl.BlockSpec((tb * rows, LANE), lambda i: (i, 0)),
        out_shape=jax.ShapeDtypeStruct((n * rows, LANE), jnp.float32),
        scratch_shapes=[pltpu.VMEM((SPLIT * tb, PEER_ROWS), jnp.float32)] + [
                        pltpu.VMEM((PEER_SLOTS * SUBLANE, LANE), jnp.int32)] * PEER_BUFS,
        compiler_params=pltpu.CompilerParams(dimension_semantics=("arbitrary",), vmem_limit_bytes=VMEM_LIMIT),
        name="peer_out",
    )(off, act2, gate2, x2.reshape(n * rows, LANE), _expand_matrix(), tab_v)
    return out.reshape(n, dm)


def peer_layer(x2, gain, w_query, sub_keys, tab_u, tab_v):
    n = x2.shape[0]
    hn, offT, parT, gateT = peer_select(x2, gain, w_query, sub_keys)
    off = offT.T.reshape(n * PEER_SLOTS)
    par, gate = parT.T, gateT.T
    gate2 = jnp.stack([jnp.where(par == 0, gate, 0.0), jnp.where(par == 1, gate, 0.0)], axis=-1).reshape(n, 2 * PEER_SLOTS)
    act2 = peer_act(off, hn, tab_u)
    return peer_out(off, act2, gate2, x2, tab_v)


TRI_BASE = 8
TRI_BATCH = 32


def _bmm(a, b):
    return jnp.einsum("gij,gjk->gik", a, b, precision=lax.Precision.HIGHEST, preferred_element_type=jnp.float32)


def _tri_inv_kernel(a_ref, t_ref):
    a = a_ref[...]
    g, n, _ = a.shape
    i = lax.broadcasted_iota(jnp.int32, (g, n, n), 1)
    j = lax.broadcasted_iota(jnp.int32, (g, n, n), 2)
    nb = jnp.where((i // TRI_BASE == j // TRI_BASE) & (i > j), a, 0.0)
    x = (i == j).astype(jnp.float32) - nb
    p = nb
    for _ in range(TRI_BASE.bit_length() - 2):
        p = _bmm(p, p)
        x = x + _bmm(x, p)
    half = TRI_BASE
    while half < n:
        blk = 2 * half
        m = jnp.where((i // blk == j // blk) & (i % blk >= half) & (j % blk < half), a, 0.0)
        x = x - _bmm(_bmm(x, m), x)
        half = blk
    t_ref[...] = x


def tri_inverse(a_mat):
    shp = a_mat.shape
    n = shp[-1]
    a3 = a_mat.reshape(-1, n, n)
    g = a3.shape[0]
    gb = TRI_BATCH if g % TRI_BATCH == 0 else 1
    out = pl.pallas_call(
        _tri_inv_kernel,
        grid=(g // gb,),
        in_specs=[pl.BlockSpec((gb, n, n), lambda b: (b, 0, 0))],
        out_specs=pl.BlockSpec((gb, n, n), lambda b: (b, 0, 0)),
        out_shape=jax.ShapeDtypeStruct((g, n, n), jnp.float32),
        compiler_params=pltpu.CompilerParams(dimension_semantics=("arbitrary",), vmem_limit_bytes=VMEM_LIMIT),
        name="tri_inverse",
    )(a3)
    return out.reshape(shp)


def rms_norm(x, g):
    xf = x.astype(jnp.float32)
    y = xf * lax.rsqrt(jnp.mean(xf * xf, axis=-1, keepdims=True) + EPS)
    return (y * g.astype(jnp.float32)).astype(x.dtype)


def l2_norm(x):
    xf = x.astype(jnp.float32)
    return xf * lax.rsqrt(jnp.sum(xf * xf, axis=-1, keepdims=True) + EPS)


def causal_dwconv(x, w):
    k = w.shape[0]
    return lax.conv_general_dilated(
        x, w[:, None, :].astype(x.dtype), window_strides=(1,), padding=[(k - 1, 0)],
        dimension_numbers=("NWC", "WIO", "NWC"), feature_group_count=x.shape[-1])


def to_chunks(t):
    bsz, seq, nh = t.shape[:3]
    t = t.reshape((bsz, seq // CHUNK, CHUNK, nh) + t.shape[3:])
    return jnp.moveaxis(jnp.moveaxis(t, 3, 2), 1, 0)


def from_chunks(t):
    nc, bsz, nh, q = t.shape[:4]
    t = jnp.moveaxis(jnp.moveaxis(t, 0, 1), 2, 3)
    return t.reshape((bsz, nc * q, nh) + t.shape[4:])


def ssd_scan(xs, dt, a, bm, cm):
    bsz, seq, nh, hp = xs.shape
    nc = seq // CHUNK
    xc = xs.reshape(bsz, nc, CHUNK, nh, hp)
    bc = bm.reshape(bsz, nc, CHUNK, nh, -1)
    cc = cm.reshape(bsz, nc, CHUNK, nh, -1)
    dtc = jnp.moveaxis(dt.reshape(bsz, nc, CHUNK, nh), 2, 3)
    acum = jnp.cumsum(dtc * a[:, None], axis=-1)
    causal = jnp.tril(jnp.ones((CHUNK, CHUNK), bool))
    seg = jnp.exp(jnp.where(causal, acum[..., :, None] - acum[..., None, :], -jnp.inf))
    cb = jnp.einsum("bcthn,bcshn->bchts", cc, bc)
    y_diag = jnp.einsum("bchts,bcshp->bcthp", cb * seg * dtc[..., None, :], xc)
    decay_end = jnp.exp(acum[..., -1:] - acum) * dtc
    states = jnp.einsum("bcshn,bchs,bcshp->bchpn", bc, decay_end, xc)
    chunk_decay = jnp.exp(acum[..., -1])

    def step(h, inp):
        st, dec = inp
        return h * dec[..., None, None] + st, h

    h0 = jnp.zeros((bsz, nh, hp, bc.shape[-1]), states.dtype)
    _, h_in = lax.scan(step, h0, (jnp.moveaxis(states, 1, 0), jnp.moveaxis(chunk_decay, 1, 0)))
    h_in = jnp.moveaxis(h_in, 0, 1)
    y_off = jnp.einsum("bcthn,bchpn,bcht->bcthp", cc, h_in, jnp.exp(acum))
    return (y_diag + y_off).reshape(bsz, seq, nh, hp)


def ssd_mixer(z, xbc, dt, conv_w, conv_b, dt_bias, a_log, d_skip, norm_g):
    bsz, seq, _ = z.shape
    f32 = jnp.float32
    xbc = jax.nn.silu(causal_dwconv(xbc, conv_w) + conv_b).astype(f32)
    gs = SSD_GROUPS * SSD_STATE
    rep = SSD_HEADS // SSD_GROUPS
    xs = xbc[..., :SSD_INNER].reshape(bsz, seq, SSD_HEADS, SSD_HEAD_DIM)
    bm = jnp.repeat(xbc[..., SSD_INNER:SSD_INNER + gs].reshape(bsz, seq, SSD_GROUPS, SSD_STATE), rep, axis=2)
    cm = jnp.repeat(xbc[..., SSD_INNER + gs:].reshape(bsz, seq, SSD_GROUPS, SSD_STATE), rep, axis=2)
    dt = jax.nn.softplus(dt.astype(f32) + dt_bias.astype(f32))
    a = -jnp.exp(a_log.astype(f32))
    y = ssd_scan(xs, dt, a, bm, cm) + xs * d_skip.astype(f32)[:, None]
    y = y.reshape(bsz, seq, SSD_INNER) * jax.nn.silu(z.astype(f32))
    return rms_norm(y, norm_g).astype(z.dtype)


def gated_delta_scan(q, k, v, beta, g):
    bsz, seq, nh, dk = q.shape
    dv = v.shape[-1]
    qc, kc, vc = to_chunks(q), to_chunks(k), to_chunks(v)
    bc, gcum = to_chunks(beta), jnp.cumsum(to_chunks(g), axis=-1)
    incl = jnp.tril(jnp.ones((CHUNK, CHUNK), bool))
    strict = jnp.tril(jnp.ones((CHUNK, CHUNK), bool), k=-1)
    decay = jnp.exp(jnp.where(incl, gcum[..., :, None] - gcum[..., None, :], -jnp.inf))
    kb = kc * bc[..., None]
    a_mat = jnp.where(strict, jnp.einsum("nbhid,nbhjd->nbhij", kb, kc) * decay, 0.0)
    t_mat = tri_inverse(a_mat)
    u = jnp.einsum("nbhij,nbhjd->nbhid", t_mat, vc * bc[..., None])
    w = jnp.einsum("nbhij,nbhjd->nbhid", t_mat, kb * jnp.exp(gcum)[..., None])
    qk = jnp.where(incl, jnp.einsum("nbhid,nbhjd->nbhij", qc, kc) * decay, 0.0)
    q_dec = qc * jnp.exp(gcum)[..., None]
    k_dec = kc * jnp.exp(gcum[..., -1:] - gcum)[..., None]
    last = jnp.exp(gcum[..., -1])

    def step(s, inp):
        u_c, w_c, qk_c, qd_c, kd_c, l_c = inp
        v_new = u_c - jnp.einsum("bhid,bhde->bhie", w_c, s)
        o = jnp.einsum("bhid,bhde->bhie", qd_c, s) + jnp.einsum("bhij,bhje->bhie", qk_c, v_new)
        s = s * l_c[..., None, None] + jnp.einsum("bhjd,bhje->bhde", kd_c, v_new)
        return s, o

    s0 = jnp.zeros((bsz, nh, dk, dv), jnp.float32)
    _, o = lax.scan(step, s0, (u, w, qk, q_dec, k_dec, last))
    return from_chunks(o)


def gdn_mixer(qkv, a, b, gate, conv_w, a_log, dt_bias, norm_g):
    bsz, seq, _ = qkv.shape
    f32 = jnp.float32
    out_dtype = qkv.dtype
    shp = (bsz, seq, GDN_HEADS, GDN_HEAD_DIM)
    qkv = jax.nn.silu(causal_dwconv(qkv, conv_w)).astype(f32)
    q = l2_norm(qkv[..., :GDN_INNER].reshape(shp)) * GDN_HEAD_DIM ** -0.5
    k = l2_norm(qkv[..., GDN_INNER:2 * GDN_INNER].reshape(shp))
    v = qkv[..., 2 * GDN_INNER:].reshape(shp)
    beta = jax.nn.sigmoid(b.astype(f32))
    g = -jnp.exp(a_log.astype(f32)) * jax.nn.softplus(a.astype(f32) + dt_bias.astype(f32))
    o = gated_delta_scan(q, k, v, beta, g)
    o = rms_norm(o, norm_g) * jax.nn.silu(gate.astype(f32).reshape(shp))
    return o.reshape(bsz, seq, GDN_INNER).astype(out_dtype)


def gla_scan(q, k, v, gk):
    bsz, seq, nh, dk = q.shape
    dv = v.shape[-1]
    gcum = jnp.cumsum(to_chunks(gk), axis=-2)
    incl = jnp.tril(jnp.ones((CHUNK, CHUNK), bool))[:, :, None]

    def step(s, inp):
        qc, kc, vc, gc = inp
        diff = gc[:, :, :, None, :] - gc[:, :, None, :, :]
        dec = jnp.exp(jnp.where(incl, diff, -jnp.inf))
        att = jnp.einsum("bhtd,bhsd,bhtsd->bhts", qc, kc, dec)
        o = jnp.einsum("bhts,bhse->bhte", att, vc) + jnp.einsum("bhtd,bhde->bhte", qc * jnp.exp(gc), s)
        s = s * jnp.exp(gc[:, :, -1, :])[..., None] + jnp.einsum(
            "bhsd,bhse->bhde", kc * jnp.exp(gc[:, :, -1:, :] - gc), vc)
        return s, o

    s0 = jnp.zeros((bsz, nh, dk, dv), jnp.float32)
    _, o = lax.scan(step, s0, (to_chunks(q), to_chunks(k), to_chunks(v), gcum))
    return from_chunks(o)


def gla_mixer(q, k, v, glr, r, w_gate2, b_gate, norm_g):
    bsz, seq, _ = q.shape
    f32 = jnp.float32
    kshape = (bsz, seq, GLA_HEADS, GLA_KEY_DIM)
    vshape = (bsz, seq, GLA_HEADS, GLA_VAL_DIM)
    gk = jax.nn.log_sigmoid((glr @ w_gate2 + b_gate).astype(f32)) / GLA_GATE_NORMALIZER
    o = gla_scan(q.astype(f32).reshape(kshape) * GLA_KEY_DIM ** -0.5, k.astype(f32).reshape(kshape),
                 v.astype(f32).reshape(vshape), gk.reshape(kshape))
    o = rms_norm(o, norm_g) * jax.nn.silu(r.astype(f32).reshape(vshape))
    return o.reshape(bsz, seq, GLA_VAL_WIDTH).astype(q.dtype)


def memory_xattn_core(q, m, wk, wv, q_norm, k_norm):
    bsz, seq, _ = q.shape
    nm = m.shape[1]
    q = rms_norm(q.reshape(bsz, seq, MEM_HEADS, MEM_HEAD_DIM), q_norm)
    k = rms_norm((m @ wk).reshape(bsz, nm, MEM_HEADS, MEM_HEAD_DIM), k_norm)
    v = (m @ wv).reshape(bsz, nm, MEM_HEADS, MEM_HEAD_DIM)
    s = jnp.einsum("bthd,bshd->bhts", q, k).astype(jnp.float32) * MEM_HEAD_DIM ** -0.5
    p = jax.nn.softmax(s, axis=-1).astype(v.dtype)
    return jnp.einsum("bhts,bshd->bthd", p, v).reshape(bsz, seq, MEM_WIDTH)


def kernel(x, mem, mix_norm, w_in, ssd_conv_w, ssd_conv_b, ssd_dt_bias, ssd_a_log, ssd_d, ssd_norm,
           gdn_conv_w, gdn_a_log, gdn_dt_bias, gdn_norm, gla_w_gate, gla_b_gate, gla_norm,
           dsa_q_norm, dsa_k_norm, idx_k_norm, w_out, xattn_norm, mem_norm, xattn_wq, xattn_wk,
           xattn_wv, xattn_wo, xattn_q_norm, xattn_k_norm, ffn_norm, peer_w_query, peer_sub_keys,
           peer_u, peer_v):
    bsz, seq, dm = x.shape
    depth = w_in.shape[0]
    splits = np.cumsum(np.array(IN_SIZES))[:-1].tolist()
    x2 = x.reshape(bsz * seq, dm)
    for l in range(depth):
        p = _matmul(x2, w_in[l], gain=mix_norm[l]).reshape(bsz, seq, IN_WIDTH)
        (ssd_z, ssd_xbc, ssd_dt, gdn_qkv, gdn_a, gdn_b, gdn_g, gla_q, gla_k, gla_v, gla_glr, gla_r,
         dsa_q, dsa_k, dsa_v, idx_q, idx_k, idx_w) = jnp.split(p, splits, axis=-1)
        y_a = ssd_mixer(ssd_z, ssd_xbc, ssd_dt, ssd_conv_w[l], ssd_conv_b[l], ssd_dt_bias[l],
                        ssd_a_log[l], ssd_d[l], ssd_norm[l])
        y_b = gdn_mixer(gdn_qkv, gdn_a, gdn_b, gdn_g, gdn_conv_w[l], gdn_a_log[l], gdn_dt_bias[l], gdn_norm[l])
        y_c = gla_mixer(gla_q, gla_k, gla_v, gla_glr, gla_r, gla_w_gate[l], gla_b_gate[l], gla_norm[l])
        y_d = dsa_mixer(dsa_q, dsa_k, dsa_v, idx_q, idx_k, idx_w, dsa_q_norm[l], dsa_k_norm[l], idx_k_norm[l])
        y = jnp.concatenate([y_a, y_b, y_c, y_d], axis=-1).reshape(bsz * seq, -1)
        x2 = _matmul(y, w_out[l], residual=x2)
        q = _matmul(x2, xattn_wq[l], gain=xattn_norm[l]).reshape(bsz, seq, MEM_WIDTH)
        o = memory_xattn_core(q, rms_norm(mem, mem_norm[l]), xattn_wk[l], xattn_wv[l],
                              xattn_q_norm[l], xattn_k_norm[l])
        x2 = _matmul(o.reshape(bsz * seq, MEM_WIDTH), xattn_wo[l], residual=x2)
        x2 = peer_layer(x2, ffn_norm[l], peer_w_query[l], peer_sub_keys[l],
                        pack_table(peer_u[l]), pack_table(peer_v[l]))
    return x2.reshape(bsz, seq, dm)
```

```python
import functools

import jax
import jax.numpy as jnp
import numpy as np
from jax import lax
from jax.experimental import pallas as pl
from jax.experimental.pallas import tpu as pltpu

CHUNK = 64
EPS = 1e-6
SSD_HEADS, SSD_HEAD_DIM, SSD_GROUPS, SSD_STATE, SSD_CONV = 4, 64, 2, 64, 4
SSD_INNER = SSD_HEADS * SSD_HEAD_DIM
SSD_CONV_DIM = SSD_INNER + 2 * SSD_GROUPS * SSD_STATE
GDN_HEADS, GDN_HEAD_DIM, GDN_CONV = 4, 64, 4
GDN_INNER = GDN_HEADS * GDN_HEAD_DIM
GLA_HEADS, GLA_KEY_DIM, GLA_VAL_DIM, GLA_GATE_RANK = 4, 32, 64, 16
GLA_KEY_WIDTH = GLA_HEADS * GLA_KEY_DIM
GLA_VAL_WIDTH = GLA_HEADS * GLA_VAL_DIM
GLA_GATE_NORMALIZER = 16.0
DSA_HEADS, DSA_HEAD_DIM, IDX_HEADS, IDX_DIM = 4, 64, 8, 32
DSA_TOPK_MAX, DSA_QBLOCK = 256, 128
MEM_HEADS, MEM_HEAD_DIM = 4, 64
MEM_WIDTH = MEM_HEADS * MEM_HEAD_DIM
PEER_HEADS, PEER_NKEYS, PEER_QDIM, PEER_TOPK = 8, 128, 128, 16
PEER_HALF = PEER_QDIM // 2
PEER_SLOTS = PEER_HEADS * PEER_TOPK

IN_SIZES = (
    SSD_INNER, SSD_CONV_DIM, SSD_HEADS,
    3 * GDN_INNER, GDN_HEADS, GDN_HEADS, GDN_INNER,
    GLA_KEY_WIDTH, GLA_KEY_WIDTH, GLA_VAL_WIDTH, GLA_GATE_RANK, GLA_VAL_WIDTH,
    DSA_HEADS * DSA_HEAD_DIM, DSA_HEAD_DIM, DSA_HEAD_DIM,
    IDX_HEADS * IDX_DIM, IDX_DIM, IDX_HEADS,
)
IN_WIDTH = sum(IN_SIZES)

LANE, SUBLANE = 128, 8
VMEM_LIMIT = 48 * 1024 * 1024
INT_MIN = -2 ** 31


def _round_up(n, m):
    return (n + m - 1) // m * m


def _mm_kernel(x_ref, g_ref, w_ref, *rest, normalize, has_res):
    if has_res:
        r_ref, o_ref, xs_ref = rest
    else:
        o_ref, xs_ref = rest

    @pl.when(pl.program_id(1) == 0)
    def _():
        xf = x_ref[...]
        if normalize:
            xf = xf * lax.rsqrt(jnp.mean(xf * xf, axis=-1, keepdims=True) + EPS) * g_ref[...]
        xs_ref[...] = xf.astype(jnp.bfloat16)

    acc = jnp.dot(xs_ref[...], w_ref[...], preferred_element_type=jnp.float32)
    if has_res:
        acc = acc + r_ref[...]
    o_ref[...] = acc


MM_MAX_TN = 1792


def _matmul(x, w, gain=None, residual=None, tm=512):
    m, k = x.shape
    n = w.shape[1]
    n_blocks = -(-_round_up(n, LANE) // MM_MAX_TN)
    tn = _round_up(-(-n // n_blocks), LANE)
    n_pad = tn * n_blocks
    wb = w.astype(jnp.bfloat16)
    if n_pad != n:
        wb = jnp.pad(wb, ((0, 0), (0, n_pad - n)))
    normalize = gain is not None
    g = (gain if normalize else jnp.ones((k,), jnp.float32)).reshape(1, k).astype(jnp.float32)
    has_res = residual is not None
    assert m % tm == 0
    in_specs = [
        pl.BlockSpec((tm, k), lambda i, j: (i, 0)),
        pl.BlockSpec((1, k), lambda i, j: (0, 0)),
        pl.BlockSpec((k, tn), lambda i, j: (0, j)),
    ]
    args = [x, g, wb]
    if has_res:
        assert n_pad == n
        in_specs.append(pl.BlockSpec((tm, tn), lambda i, j: (i, j)))
        args.append(residual)
    out = pl.pallas_call(
        functools.partial(_mm_kernel, normalize=normalize, has_res=has_res),
        grid=(m // tm, n_pad // tn),
        in_specs=in_specs,
        out_specs=pl.BlockSpec((tm, tn), lambda i, j: (i, j)),
        out_shape=jax.ShapeDtypeStruct((m, n_pad), jnp.float32),
        scratch_shapes=[pltpu.VMEM((tm, k), jnp.bfloat16)],
        compiler_params=pltpu.CompilerParams(
            dimension_semantics=("arbitrary", "arbitrary"), vmem_limit_bytes=VMEM_LIMIT),
        name="norm_matmul",
    )(*args)
    return out[:, :n] if n_pad != n else out


def _dsa_kernel(qT_ref, k_ref, vT_ref, iqT_ref, ik_ref, iwT_ref, qg_ref, kg_ref, ikg_ref, o_ref,
                qn_s, kn_s, ikn_s, iq_s, vT_s, key_s, *, seq, topk):
    f32, bf16, i32 = jnp.float32, jnp.bfloat16, jnp.int32
    nq = DSA_QBLOCK
    for h in range(DSA_HEADS):
        rows = slice(h * DSA_HEAD_DIM, (h + 1) * DSA_HEAD_DIM)
        qh = qT_ref[rows, :]
        ms = jnp.mean(qh * qh, axis=0, keepdims=True)
        qn_s[rows, :] = (qh * lax.rsqrt(ms + EPS) * qg_ref[...]).astype(bf16)
    kk = k_ref[...]
    kn_s[...] = (kk * lax.rsqrt(jnp.mean(kk * kk, axis=-1, keepdims=True) + EPS) * kg_ref[...]).astype(bf16)
    ikk = ik_ref[...]
    ikn_s[...] = (ikk * lax.rsqrt(jnp.mean(ikk * ikk, axis=-1, keepdims=True) + EPS) * ikg_ref[...]).astype(bf16)
    iq_s[...] = iqT_ref[...].astype(bf16)
    vT_s[...] = vT_ref[...].astype(bf16)
    iw_scale = IDX_HEADS ** -0.5 * IDX_DIM ** -0.5
    n_jbits = max(1, int(seq).bit_length())

    def count(mask):
        return jnp.sum(mask.astype(i32), axis=0, keepdims=True)

    for i in range(seq // nq):
        nk = nq * (i + 1)
        qs = slice(i * nq, (i + 1) * nq)
        kidx = lax.broadcasted_iota(i32, (nk, nq), 0)
        lane = lax.broadcasted_iota(i32, (nk, nq), 1)
        valid = (kidx < i * nq + CHUNK) | (lane >= CHUNK)
        if nk > topk:
            sc = jnp.zeros((nk, nq), f32)
            for h in range(IDX_HEADS):
                a = jnp.dot(ikn_s[:nk, :], iq_s[h * IDX_DIM:(h + 1) * IDX_DIM, qs], preferred_element_type=f32)
                sc = sc + jnp.maximum(a, 0.0) * (iwT_ref[h:h + 1, qs] * iw_scale)
            sc = jnp.where(sc == 0.0, 0.0, sc)
            bits = pltpu.bitcast(sc, i32)
            key = jnp.where(bits < 0, bits ^ 0x7FFFFFFF, bits)
            key_s[:nk, :] = jnp.where(valid, key, INT_MIN)

            t0 = jnp.where(count(key_s[:nk, :] >= 0) >= topk, 0, INT_MIN).astype(i32)

            def tbody(b, t):
                cand = t + jnp.left_shift(jnp.int32(1), 30 - b)
                return jnp.where(count(key_s[:nk, :] >= cand) >= topk, cand, t)

            thr = lax.fori_loop(0, 31, tbody, t0)
            keyv = key_s[:nk, :]
            gt = keyv > thr
            need = topk - count(gt)

            def jbody(b, j):
                cand = j + jnp.left_shift(jnp.int32(1), n_jbits - 1 - b)
                f = count((key_s[:nk, :] == thr) & (kidx < cand))
                return jnp.where(f <= need, cand, j)

            surplus = jnp.max(count(keyv == thr) - need)
            jlim = lax.cond(surplus > 0,
                            lambda: lax.fori_loop(0, n_jbits, jbody, jnp.zeros((1, nq), i32)),
                            lambda: jnp.full((1, nq), 1 << n_jbits, i32))
            sel = gt | ((keyv == thr) & (kidx < jlim))
        else:
            sel = valid
        for h in range(DSA_HEADS):
            rows = slice(h * DSA_HEAD_DIM, (h + 1) * DSA_HEAD_DIM)
            s = jnp.dot(kn_s[:nk, :], qn_s[rows, qs], preferred_element_type=f32) * DSA_HEAD_DIM ** -0.5
            s = jnp.where(sel, s, -jnp.inf)
            m = jnp.max(s, axis=0, keepdims=True)
            p = jnp.exp(s - m)
            l = jnp.sum(p, axis=0, keepdims=True)
            oT = jnp.dot(vT_s[:, :nk], p.astype(bf16), preferred_element_type=f32)
            o_ref[rows, qs] = oT / l


def dsa_attention(qT, k, vT, iqT, ik, iwT, q_norm, k_norm, ik_norm):
    bsz, _, seq = qT.shape
    assert seq % DSA_QBLOCK == 0
    topk = min(DSA_TOPK_MAX, seq // 4)
    hd, idd = DSA_HEAD_DIM, IDX_DIM
    bspec = lambda *shape: pl.BlockSpec((None,) + shape, lambda b: (b,) + (0,) * len(shape))
    cspec = lambda *shape: pl.BlockSpec(shape, lambda b: (0,) * len(shape))
    bf16 = jnp.bfloat16
    return pl.pallas_call(
        functools.partial(_dsa_kernel, seq=seq, topk=topk),
        grid=(bsz,),
        in_specs=[bspec(DSA_HEADS * hd, seq), bspec(seq, hd), bspec(hd, seq), bspec(IDX_HEADS * idd, seq),
                  bspec(seq, idd), bspec(IDX_HEADS, seq), cspec(hd, 1), cspec(1, hd), cspec(1, idd)],
        out_specs=bspec(DSA_HEADS * hd, seq),
        out_shape=jax.ShapeDtypeStruct((bsz, DSA_HEADS * hd, seq), jnp.float32),
        scratch_shapes=[pltpu.VMEM((DSA_HEADS * hd, seq), bf16), pltpu.VMEM((seq, hd), bf16),
                        pltpu.VMEM((seq, idd), bf16), pltpu.VMEM((IDX_HEADS * idd, seq), bf16),
                        pltpu.VMEM((hd, seq), bf16), pltpu.VMEM((seq, DSA_QBLOCK), jnp.int32)],
        compiler_params=pltpu.CompilerParams(dimension_semantics=("arbitrary",), vmem_limit_bytes=VMEM_LIMIT),
        name="dsa_attention",
    )(qT, k, vT, iqT, ik, iwT, q_norm.reshape(hd, 1), k_norm.reshape(1, hd), ik_norm.reshape(1, idd))


def dsa_mixer(q, k, v, iq, ik, iw, q_norm, k_norm, ik_norm):
    t = lambda a: jnp.swapaxes(a, 1, 2)
    return t(dsa_attention(t(q), k, t(v), t(iq), ik, t(iw), q_norm, k_norm, ik_norm))


def _top16(vals, payload=None):
    n = vals.shape[0]
    iota = lax.broadcasted_iota(jnp.int32, vals.shape, 0)
    tv, ti = [], []
    for _ in range(PEER_TOPK):
        m = jnp.max(vals, axis=0, keepdims=True)
        idx = jnp.min(jnp.where(vals == m, iota, n), axis=0, keepdims=True)
        hit = iota == idx
        tv.append(m)
        ti.append(idx if payload is None else jnp.max(jnp.where(hit, payload, -1), axis=0, keepdims=True))
        vals = jnp.where(hit, -jnp.inf, vals)
    return jnp.concatenate(tv, axis=0), jnp.concatenate(ti, axis=0)


def _candidates(s1, i1, s2, i2):
    sub = lax.broadcasted_iota(jnp.int32, (SUBLANE, LANE), 0)
    cv, ci = [s1[0:1] + s2], [i1[0:1] * PEER_NKEYS + i2]
    for i in range(1, SUBLANE):
        v = s1[i:i + 1] + s2[0:SUBLANE]
        cv.append(jnp.where(sub < PEER_TOPK // (i + 1), v, -jnp.inf))
        ci.append(i1[i:i + 1] * PEER_NKEYS + i2[0:SUBLANE])
    cv.append(s1[SUBLANE:] + s2[0:1])
    ci.append(i1[SUBLANE:] * PEER_NKEYS + i2[0:1])
    return jnp.concatenate(cv, axis=0), jnp.concatenate(ci, axis=0)


def _peer_select_kernel(x_ref, g_ref, wqT_ref, sk_ref, hn_ref, off_ref, par_ref, gate_ref, qT_s, *, tb):
    f32, bf16 = jnp.float32, jnp.bfloat16
    x = x_ref[...]
    hn = x * lax.rsqrt(jnp.mean(x * x, axis=-1, keepdims=True) + EPS) * g_ref[...]
    hn_ref[...] = hn
    qT_s[...] = lax.dot_general(wqT_ref[...], hn.astype(bf16), (((1,), (1,)), ((), ())),
                                preferred_element_type=f32)

    def head(h, carry):
        for ct in range(tb // LANE):
            cols = slice(ct * LANE, (ct + 1) * LANE)
            tops = []
            for p in range(2):
                hp = h * 2 + p
                qhp = qT_s[pl.ds(pl.multiple_of(hp * PEER_HALF, PEER_HALF), PEER_HALF), cols]
                s = jnp.dot(sk_ref[hp], qhp.astype(bf16), preferred_element_type=f32)
                tops.append(_top16(s))
            (s1, i1), (s2, i2) = tops
            cand, cidx = _candidates(s1, i1, s2, i2)
            top, eidx = _top16(cand, cidx)
            g = jnp.exp(top - top[0:1])
            g = g / jnp.sum(g, axis=0, keepdims=True)
            rows = pl.ds(pl.multiple_of(h * PEER_TOPK, PEER_TOPK), PEER_TOPK)
            off_ref[rows, cols] = (eidx >> 1) * SUBLANE
            par_ref[rows, cols] = eidx & 1
            gate_ref[rows, cols] = g
        return carry

    lax.fori_loop(0, PEER_HEADS, head, 0)


def peer_select(x2, gain, w_query, sub_keys, tb=256):
    n, dm = x2.shape
    nq = w_query.shape[1]
    wqT = w_query.T.astype(jnp.bfloat16)
    sk = sub_keys.reshape(PEER_HEADS * 2, PEER_NKEYS, PEER_HALF).astype(jnp.bfloat16)
    return pl.pallas_call(
        functools.partial(_peer_select_kernel, tb=tb),
        grid=(n // tb,),
        in_specs=[pl.BlockSpec((tb, dm), lambda i: (i, 0)), pl.BlockSpec((1, dm), lambda i: (0, 0)),
                  pl.BlockSpec((nq, dm), lambda i: (0, 0)),
                  pl.BlockSpec((PEER_HEADS * 2, PEER_NKEYS, PEER_HALF), lambda i: (0, 0, 0))],
        out_specs=[pl.BlockSpec((tb, dm), lambda i: (i, 0))] + [pl.BlockSpec((PEER_SLOTS, tb), lambda i: (0, i))] * 3,
        out_shape=[jax.ShapeDtypeStruct((n, dm), jnp.float32), jax.ShapeDtypeStruct((PEER_SLOTS, n), jnp.int32),
                   jax.ShapeDtypeStruct((PEER_SLOTS, n), jnp.int32), jax.ShapeDtypeStruct((PEER_SLOTS, n), jnp.float32)],
        scratch_shapes=[pltpu.VMEM((nq, tb), jnp.float32)],
        compiler_params=pltpu.CompilerParams(dimension_semantics=("arbitrary",), vmem_limit_bytes=VMEM_LIMIT),
        name="peer_select",
    )(x2, gain.reshape(1, dm), wqT, sk)


def pack_table(tab):
    e, dm = tab.shape
    b = lax.bitcast_convert_type(tab.astype(jnp.bfloat16), jnp.uint16).astype(jnp.uint32)
    packed = (b[1::2] << 16) | b[0::2]
    return lax.bitcast_convert_type(packed, jnp.int32).reshape(e // 2 * (dm // LANE), LANE)


SPLIT = 3
PEER_BUFS = 4


def _gather_tiles(off_ref, tab_ref, g_ref, t):
    offs = off_ref.at[pl.ds(t * PEER_SLOTS, PEER_SLOTS)]
    for s in range(PEER_SLOTS):
        g_ref[s * SUBLANE:(s + 1) * SUBLANE, :] = tab_ref[pl.ds(pl.multiple_of(offs[s], SUBLANE), SUBLANE), :]


PEER_ROWS = 2 * PEER_SLOTS * SUBLANE


def _expand_matrix():
    k = np.arange(PEER_ROWS)
    c = np.arange(2 * PEER_SLOTS)
    return jnp.asarray(c[:, None] == (2 * (k // (2 * SUBLANE)) + k % 2)[None, :], jnp.bfloat16)


def _own_sublane(nrows):
    row = lax.broadcasted_iota(jnp.int32, (nrows, PEER_ROWS), 0) % SUBLANE
    col = lax.broadcasted_iota(jnp.int32, (nrows, PEER_ROWS), 1)
    return (col // 2) % SUBLANE == row


def _split_bf16(x):
    parts = []
    for _ in range(SPLIT):
        p = x.astype(jnp.bfloat16)
        parts.append(p)
        x = x - p.astype(jnp.float32)
    return parts


def _tile_rows(g_ref):
    return pltpu.bitcast(g_ref[...], jnp.bfloat16)


def _pipelined_tokens(off_ref, tab_ref, bufs, dots, tb):
    def token_quad(i, carry):
        t = 4 * i
        dots(jnp.maximum(t - 2, 0), bufs[2])
        dots(jnp.maximum(t - 1, 1), bufs[3])
        _gather_tiles(off_ref, tab_ref, bufs[0], t)
        _gather_tiles(off_ref, tab_ref, bufs[1], t + 1)
        dots(t, bufs[0])
        dots(t + 1, bufs[1])
        _gather_tiles(off_ref, tab_ref, bufs[2], t + 2)
        _gather_tiles(off_ref, tab_ref, bufs[3], t + 3)
        return carry

    lax.fori_loop(0, tb // PEER_BUFS, token_quad, 0)
    dots(tb - 2, bufs[2])
    dots(tb - 1, bufs[3])


def _zero_stale_buffers(bufs):
    @pl.when(pl.program_id(0) == 0)
    def _():
        for buf in bufs[2:]:
            buf[...] = jnp.zeros(buf.shape, buf.dtype)


def _peer_act_kernel(off_ref, h_ref, e_ref, tab_ref, act_ref, a_s, *bufs, tb):
    f32 = jnp.float32
    nt = (((1,), (1,)), ((), ()))
    mine = _own_sublane(SPLIT * SUBLANE)
    _zero_stale_buffers(bufs)

    def dots(t, buf):
        ht = h_ref[pl.ds(pl.multiple_of(t * SUBLANE, SUBLANE), SUBLANE), :]
        lhs = jnp.concatenate(_split_bf16(ht), axis=0)
        res = lax.dot_general(lhs, _tile_rows(buf), nt, preferred_element_type=f32)
        a_s[pl.ds(t, 1), :] = jnp.sum(jnp.where(mine, res, 0.0), axis=0, keepdims=True)

    _pipelined_tokens(off_ref, tab_ref, bufs, dots, tb)
    parts = lax.dot_general(jnp.concatenate(_split_bf16(a_s[...]), axis=0), e_ref[...], nt, preferred_element_type=f32)
    act_ref[...] = (parts[0:tb] + parts[tb:2 * tb]) + parts[2 * tb:]


def peer_act(off, hn, tab_u, tb=128):
    n, dm = hn.shape
    rows = dm // LANE
    assert tb % PEER_BUFS == 0 and n % tb == 0
    return pl.pallas_call(
        functools.partial(_peer_act_kernel, tb=tb),
        grid=(n // tb,),
        in_specs=[pl.BlockSpec((tb * PEER_SLOTS,), lambda i: (i,), memory_space=pltpu.SMEM),
                  pl.BlockSpec((tb * rows, LANE), lambda i: (i, 0)),
                  pl.BlockSpec((2 * PEER_SLOTS, PEER_ROWS), lambda i: (0, 0)),
                  pl.BlockSpec(tab_u.shape, lambda i: (0, 0), pipeline_mode=pl.Buffered(1))],
        out_specs=pl.BlockSpec((tb, 2 * PEER_SLOTS), lambda i: (i, 0)),
        out_shape=jax.ShapeDtypeStruct((n, 2 * PEER_SLOTS), jnp.float32),
        scratch_shapes=[pltpu.VMEM((tb, PEER_ROWS), jnp.float32)] + [
                        pltpu.VMEM((PEER_SLOTS * SUBLANE, LANE), jnp.int32)] * PEER_BUFS,
        compiler_params=pltpu.CompilerParams(dimension_semantics=("arbitrary",), vmem_limit_bytes=VMEM_LIMIT),
        name="peer_act",
    )(off, hn.reshape(n * rows, LANE), _expand_matrix(), tab_u)


def _peer_out_kernel(off_ref, act_ref, gate_ref, x_ref, e_ref, tab_ref, o_ref, cexp_s, *bufs, tb):
    f32 = jnp.float32
    mine = _own_sublane(SUBLANE)
    _zero_stale_buffers(bufs)
    c = gate_ref[...] * jax.nn.gelu(act_ref[...])
    cexp_s[...] = jnp.dot(jnp.concatenate(_split_bf16(c), axis=0), e_ref[...], preferred_element_type=f32)

    def dots(t, buf):
        lhs = jnp.concatenate([jnp.where(mine, cexp_s[pl.ds(part * tb + t, 1), :], 0.0) for part in range(SPLIT)], axis=0)
        acc = jnp.dot(lhs.astype(jnp.bfloat16), _tile_rows(buf), preferred_element_type=f32)
        rows = pl.ds(pl.multiple_of(t * SUBLANE, SUBLANE), SUBLANE)
        o_ref[rows, :] = x_ref[rows, :] + ((acc[0:SUBLANE] + acc[SUBLANE:2 * SUBLANE]) + acc[2 * SUBLANE:])

    _pipelined_tokens(off_ref, tab_ref, bufs, dots, tb)


def peer_out(off, act2, gate2, x2, tab_v, tb=128):
    n, dm = x2.shape
    rows = dm // LANE
    assert tb % PEER_BUFS == 0 and n % tb == 0
    out = pl.pallas_call(
        functools.partial(_peer_out_kernel, tb=tb),
        grid=(n // tb,),
        in_specs=[pl.BlockSpec((tb * PEER_SLOTS,), lambda i: (i,), memory_space=pltpu.SMEM),
                  pl.BlockSpec((tb, 2 * PEER_SLOTS), lambda i: (i, 0)),
                  pl.BlockSpec((tb, 2 * PEER_SLOTS), lambda i: (i, 0)),
                  pl.BlockSpec((tb * rows, LANE), lambda i: (i, 0)),
                  pl.BlockSpec((2 * PEER_SLOTS, PEER_ROWS), lambda i: (0, 0)),
                  pl.BlockSpec(tab_v.shape, lambda i: (0, 0), pipeline_mode=pl.Buffered(1))],
        out_specs=pl.BlockSpec((tb * rows, LANE), lambda i: (i, 0)),
        out_shape=jax.ShapeDtypeStruct((n * rows, LANE), jnp.float32),
        scratch_shapes=[pltpu.VMEM((SPLIT * tb, PEER_ROWS), jnp.float32)] + [
                        pltpu.VMEM((PEER_SLOTS * SUBLANE, LANE), jnp.int32)] * PEER_BUFS,
        compiler_params=pltpu.CompilerParams(dimension_semantics=("arbitrary",), vmem_limit_bytes=VMEM_LIMIT),
        name="peer_out",
    )(off, act2, gate2, x2.reshape(n * rows, LANE), _expand_matrix(), tab_v)
    return out.reshape(n, dm)


def peer_layer(x2, gain, w_query, sub_keys, tab_u, tab_v):
    n = x2.shape[0]
    hn, offT, parT, gateT = peer_select(x2, gain, w_query, sub_keys)
    off = offT.T.reshape(n * PEER_SLOTS)
    par, gate = parT.T, gateT.T
    gate2 = jnp.stack([jnp.where(par == 0, gate, 0.0), jnp.where(par == 1, gate, 0.0)], axis=-1).reshape(n, 2 * PEER_SLOTS)
    act2 = peer_act(off, hn, tab_u)
    return peer_out(off, act2, gate2, x2, tab_v)


TRI_BASE = 8
TRI_BATCH = 32


def _bmm(a, b):
    return jnp.einsum("gij,gjk->gik", a, b, precision=lax.Precision.HIGHEST, preferred_element_type=jnp.float32)


def _tri_inv_kernel(a_ref, t_ref):
    a = a_ref[...]
    g, n, _ = a.shape
    i = lax.broadcasted_iota(jnp.int32, (g, n, n), 1)
    j = lax.broadcasted_iota(jnp.int32, (g, n, n), 2)
    nb = jnp.where((i // TRI_BASE == j // TRI_BASE) & (i > j), a, 0.0)
    x = (i == j).astype(jnp.float32) - nb
    p = nb
    for _ in range(TRI_BASE.bit_length() - 2):
        p = _bmm(p, p)
        x = x + _bmm(x, p)
    half = TRI_BASE
    while half < n:
        blk = 2 * half
        m = jnp.where((i // blk == j // blk) & (i % blk >= half) & (j % blk < half), a, 0.0)
        x = x - _bmm(_bmm(x, m), x)
        half = blk
    t_ref[...] = x


def tri_inverse(a_mat):
    shp = a_mat.shape
    n = shp[-1]
    a3 = a_mat.reshape(-1, n, n)
    g = a3.shape[0]
    gb = TRI_BATCH if g % TRI_BATCH == 0 else 1
    out = pl.pallas_call(
        _tri_inv_kernel,
        grid=(g // gb,),
        in_specs=[pl.BlockSpec((gb, n, n), lambda b: (b, 0, 0))],
        out_specs=pl.BlockSpec((gb, n, n), lambda b: (b, 0, 0)),
        out_shape=jax.ShapeDtypeStruct((g, n, n), jnp.float32),
        compiler_params=pltpu.CompilerParams(dimension_semantics=("arbitrary",), vmem_limit_bytes=VMEM_LIMIT),
        name="tri_inverse",
    )(a3)
    return out.reshape(shp)


def rms_norm(x, g):
    xf = x.astype(jnp.float32)
    y = xf * lax.rsqrt(jnp.mean(xf * xf, axis=-1, keepdims=True) + EPS)
    return (y * g.astype(jnp.float32)).astype(x.dtype)


def l2_norm(x):
    xf = x.astype(jnp.float32)
    return xf * lax.rsqrt(jnp.sum(xf * xf, axis=-1, keepdims=True) + EPS)


def causal_dwconv(x, w):
    k = w.shape[0]
    return lax.conv_general_dilated(
        x, w[:, None, :].astype(x.dtype), window_strides=(1,), padding=[(k - 1, 0)],
        dimension_numbers=("NWC", "WIO", "NWC"), feature_group_count=x.shape[-1])


def to_chunks(t):
    bsz, seq, nh = t.shape[:3]
    t = t.reshape((bsz, seq // CHUNK, CHUNK, nh) + t.shape[3:])
    return jnp.moveaxis(jnp.moveaxis(t, 3, 2), 1, 0)


def from_chunks(t):
    nc, bsz, nh, q = t.shape[:4]
    t = jnp.moveaxis(jnp.moveaxis(t, 0, 1), 2, 3)
    return t.reshape((bsz, nc * q, nh) + t.shape[4:])


def ssd_scan(xs, dt, a, bm, cm):
    bsz, seq, nh, hp = xs.shape
    nc = seq // CHUNK
    xc = xs.reshape(bsz, nc, CHUNK, nh, hp)
    bc = bm.reshape(bsz, nc, CHUNK, nh, -1)
    cc = cm.reshape(bsz, nc, CHUNK, nh, -1)
    dtc = jnp.moveaxis(dt.reshape(bsz, nc, CHUNK, nh), 2, 3)
    acum = jnp.cumsum(dtc * a[:, None], axis=-1)
    causal = jnp.tril(jnp.ones((CHUNK, CHUNK), bool))
    seg = jnp.exp(jnp.where(causal, acum[..., :, None] - acum[..., None, :], -jnp.inf))
    cb = jnp.einsum("bcthn,bcshn->bchts", cc, bc)
    y_diag = jnp.einsum("bchts,bcshp->bcthp", cb * seg * dtc[..., None, :], xc)
    decay_end = jnp.exp(acum[..., -1:] - acum) * dtc
    states = jnp.einsum("bcshn,bchs,bcshp->bchpn", bc, decay_end, xc)
    chunk_decay = jnp.exp(acum[..., -1])

    def step(h, inp):
        st, dec = inp
        return h * dec[..., None, None] + st, h

    h0 = jnp.zeros((bsz, nh, hp, bc.shape[-1]), states.dtype)
    _, h_in = lax.scan(step, h0, (jnp.moveaxis(states, 1, 0), jnp.moveaxis(chunk_decay, 1, 0)))
    h_in = jnp.moveaxis(h_in, 0, 1)
    y_off = jnp.einsum("bcthn,bchpn,bcht->bcthp", cc, h_in, jnp.exp(acum))
    return (y_diag + y_off).reshape(bsz, seq, nh, hp)


def ssd_mixer(z, xbc, dt, conv_w, conv_b, dt_bias, a_log, d_skip, norm_g):
    bsz, seq, _ = z.shape
    f32 = jnp.float32
    xbc = jax.nn.silu(causal_dwconv(xbc, conv_w) + conv_b).astype(f32)
    gs = SSD_GROUPS * SSD_STATE
    rep = SSD_HEADS // SSD_GROUPS
    xs = xbc[..., :SSD_INNER].reshape(bsz, seq, SSD_HEADS, SSD_HEAD_DIM)
    bm = jnp.repeat(xbc[..., SSD_INNER:SSD_INNER + gs].reshape(bsz, seq, SSD_GROUPS, SSD_STATE), rep, axis=2)
    cm = jnp.repeat(xbc[..., SSD_INNER + gs:].reshape(bsz, seq, SSD_GROUPS, SSD_STATE), rep, axis=2)
    dt = jax.nn.softplus(dt.astype(f32) + dt_bias.astype(f32))
    a = -jnp.exp(a_log.astype(f32))
    y = ssd_scan(xs, dt, a, bm, cm) + xs * d_skip.astype(f32)[:, None]
    y = y.reshape(bsz, seq, SSD_INNER) * jax.nn.silu(z.astype(f32))
    return rms_norm(y, norm_g).astype(z.dtype)


def gated_delta_scan(q, k, v, beta, g):
    bsz, seq, nh, dk = q.shape
    dv = v.shape[-1]
    qc, kc, vc = to_chunks(q), to_chunks(k), to_chunks(v)
    bc, gcum = to_chunks(beta), jnp.cumsum(to_chunks(g), axis=-1)
    incl = jnp.tril(jnp.ones((CHUNK, CHUNK), bool))
    strict = jnp.tril(jnp.ones((CHUNK, CHUNK), bool), k=-1)
    decay = jnp.exp(jnp.where(incl, gcum[..., :, None] - gcum[..., None, :], -jnp.inf))
    kb = kc * bc[..., None]
    a_mat = jnp.where(strict, jnp.einsum("nbhid,nbhjd->nbhij", kb, kc) * decay, 0.0)
    t_mat = tri_inverse(a_mat)
    u = jnp.einsum("nbhij,nbhjd->nbhid", t_mat, vc * bc[..., None])
    w = jnp.einsum("nbhij,nbhjd->nbhid", t_mat, kb * jnp.exp(gcum)[..., None])
    qk = jnp.where(incl, jnp.einsum("nbhid,nbhjd->nbhij", qc, kc) * decay, 0.0)
    q_dec = qc * jnp.exp(gcum)[..., None]
    k_dec = kc * jnp.exp(gcum[..., -1:] - gcum)[..., None]
    last = jnp.exp(gcum[..., -1])

    def step(s, inp):
        u_c, w_c, qk_c, qd_c, kd_c, l_c = inp
        v_new = u_c - jnp.einsum("bhid,bhde->bhie", w_c, s)
        o = jnp.einsum("bhid,bhde->bhie", qd_c, s) + jnp.einsum("bhij,bhje->bhie", qk_c, v_new)
        s = s * l_c[..., None, None] + jnp.einsum("bhjd,bhje->bhde", kd_c, v_new)
        return s, o

    s0 = jnp.zeros((bsz, nh, dk, dv), jnp.float32)
    _, o = lax.scan(step, s0, (u, w, qk, q_dec, k_dec, last))
    return from_chunks(o)


def gdn_mixer(qkv, a, b, gate, conv_w, a_log, dt_bias, norm_g):
    bsz, seq, _ = qkv.shape
    f32 = jnp.float32
    out_dtype = qkv.dtype
    shp = (bsz, seq, GDN_HEADS, GDN_HEAD_DIM)
    qkv = jax.nn.silu(causal_dwconv(qkv, conv_w)).astype(f32)
    q = l2_norm(qkv[..., :GDN_INNER].reshape(shp)) * GDN_HEAD_DIM ** -0.5
    k = l2_norm(qkv[..., GDN_INNER:2 * GDN_INNER].reshape(shp))
    v = qkv[..., 2 * GDN_INNER:].reshape(shp)
    beta = jax.nn.sigmoid(b.astype(f32))
    g = -jnp.exp(a_log.astype(f32)) * jax.nn.softplus(a.astype(f32) + dt_bias.astype(f32))
    o = gated_delta_scan(q, k, v, beta, g)
    o = rms_norm(o, norm_g) * jax.nn.silu(gate.astype(f32).reshape(shp))
    return o.reshape(bsz, seq, GDN_INNER).astype(out_dtype)


def gla_scan(q, k, v, gk):
    bsz, seq, nh, dk = q.shape
    dv = v.shape[-1]
    gcum = jnp.cumsum(to_chunks(gk), axis=-2)
    incl = jnp.tril(jnp.ones((CHUNK, CHUNK), bool))[:, :, None]

    def step(s, inp):
        qc, kc, vc, gc = inp
        diff = gc[:, :, :, None, :] - gc[:, :, None, :, :]
        dec = jnp.exp(jnp.where(incl, diff, -jnp.inf))
        att = jnp.einsum("bhtd,bhsd,bhtsd->bhts", qc, kc, dec)
        o = jnp.einsum("bhts,bhse->bhte", att, vc) + jnp.einsum("bhtd,bhde->bhte", qc * jnp.exp(gc), s)
        s = s * jnp.exp(gc[:, :, -1, :])[..., None] + jnp.einsum(
            "bhsd,bhse->bhde", kc * jnp.exp(gc[:, :, -1:, :] - gc), vc)
        return s, o

    s0 = jnp.zeros((bsz, nh, dk, dv), jnp.float32)
    _, o = lax.scan(step, s0, (to_chunks(q), to_chunks(k), to_chunks(v), gcum))
    return from_chunks(o)


def gla_mixer(q, k, v, glr, r, w_gate2, b_gate, norm_g):
    bsz, seq, _ = q.shape
    f32 = jnp.float32
    kshape = (bsz, seq, GLA_HEADS, GLA_KEY_DIM)
    vshape = (bsz, seq, GLA_HEADS, GLA_VAL_DIM)
    gk = jax.nn.log_sigmoid((glr @ w_gate2 + b_gate).astype(f32)) / GLA_GATE_NORMALIZER
    o = gla_scan(q.astype(f32).reshape(kshape) * GLA_KEY_DIM ** -0.5, k.astype(f32).reshape(kshape),
                 v.astype(f32).reshape(vshape), gk.reshape(kshape))
    o = rms_norm(o, norm_g) * jax.nn.silu(r.astype(f32).reshape(vshape))
    return o.reshape(bsz, seq, GLA_VAL_WIDTH).astype(q.dtype)


def memory_xattn_core(q, m, wk, wv, q_norm, k_norm):
    bsz, seq, _ = q.shape
    nm = m.shape[1]
    q = rms_norm(q.reshape(bsz, seq, MEM_HEADS, MEM_HEAD_DIM), q_norm)
    k = rms_norm((m @ wk).reshape(bsz, nm, MEM_HEADS, MEM_HEAD_DIM), k_norm)
    v = (m @ wv).reshape(bsz, nm, MEM_HEADS, MEM_HEAD_DIM)
    s = jnp.einsum("bthd,bshd->bhts", q, k).astype(jnp.float32) * MEM_HEAD_DIM ** -0.5
    p = jax.nn.softmax(s, axis=-1).astype(v.dtype)
    return jnp.einsum("bhts,bshd->bthd", p, v).reshape(bsz, seq, MEM_WIDTH)


def kernel(x, mem, mix_norm, w_in, ssd_conv_w, ssd_conv_b, ssd_dt_bias, ssd_a_log, ssd_d, ssd_norm,
           gdn_conv_w, gdn_a_log, gdn_dt_bias, gdn_norm, gla_w_gate, gla_b_gate, gla_norm,
           dsa_q_norm, dsa_k_norm, idx_k_norm, w_out, xattn_norm, mem_norm, xattn_wq, xattn_wk,
           xattn_wv, xattn_wo, xattn_q_norm, xattn_k_norm, ffn_norm, peer_w_query, peer_sub_keys,
           peer_u, peer_v):
    bsz, seq, dm = x.shape
    depth = w_in.shape[0]
    splits = np.cumsum(np.array(IN_SIZES))[:-1].tolist()
    x2 = x.reshape(bsz * seq, dm)
    for l in range(depth):
        p = _matmul(x2, w_in[l], gain=mix_norm[l]).reshape(bsz, seq, IN_WIDTH)
        (ssd_z, ssd_xbc, ssd_dt, gdn_qkv, gdn_a, gdn_b, gdn_g, gla_q, gla_k, gla_v, gla_glr, gla_r,
         dsa_q, dsa_k, dsa_v, idx_q, idx_k, idx_w) = jnp.split(p, splits, axis=-1)
        y_a = ssd_mixer(ssd_z, ssd_xbc, ssd_dt, ssd_conv_w[l], ssd_conv_b[l], ssd_dt_bias[l],
                        ssd_a_log[l], ssd_d[l], ssd_norm[l])
        y_b = gdn_mixer(gdn_qkv, gdn_a, gdn_b, gdn_g, gdn_conv_w[l], gdn_a_log[l], gdn_dt_bias[l], gdn_norm[l])
        y_c = gla_mixer(gla_q, gla_k, gla_v, gla_glr, gla_r, gla_w_gate[l], gla_b_gate[l], gla_norm[l])
        y_d = dsa_mixer(dsa_q, dsa_k, dsa_v, idx_q, idx_k, idx_w, dsa_q_norm[l], dsa_k_norm[l], idx_k_norm[l])
        y = jnp.concatenate([y_a, y_b, y_c, y_d], axis=-1).reshape(bsz * seq, -1)
        x2 = _matmul(y, w_out[l], residual=x2)
        q = _matmul(x2, xattn_wq[l], gain=xattn_norm[l]).reshape(bsz, seq, MEM_WIDTH)
        o = memory_xattn_core(q, rms_norm(mem, mem_norm[l]), xattn_wk[l], xattn_wv[l],
                              xattn_q_norm[l], xattn_k_norm[l])
        x2 = _matmul(o.reshape(bsz * seq, MEM_WIDTH), xattn_wo[l], residual=x2)
        x2 = peer_layer(x2, ffn_norm[l], peer_w_query[l], peer_sub_keys[l],
                        pack_table(peer_u[l]), pack_table(peer_v[l]))
    return x2.reshape(bsz, seq, dm)
```
